```python
import math
import jax, jax.numpy as jnp
from jax import lax
import numpy as np

D_MODEL = 2048
BATCH = 2
SEQ = 8192
DEPTH = 2

ATTN_WIDTH = D_MODEL // 2
HGRN_WIDTH = D_MODEL - ATTN_WIDTH
ATTN_HEADS = 8
ATTN_HEAD_DIM = ATTN_WIDTH // (2 * ATTN_HEADS)
ATTN_V_DIM = 2 * ATTN_HEAD_DIM
HGRN_EXPAND = 128
HGRN_HEADS = HGRN_WIDTH // HGRN_EXPAND
HGRN_KEY_DIM = HGRN_EXPAND
HGRN_VAL_DIM = HGRN_WIDTH // HGRN_HEADS
D_FF = int(math.ceil(8 * D_MODEL / 3 / 256)) * 256
IN_WIDTH = 3 * ATTN_WIDTH + 4 * HGRN_WIDTH
SPLITS = (ATTN_WIDTH, 2 * ATTN_WIDTH, 3 * ATTN_WIDTH,
          3 * ATTN_WIDTH + HGRN_WIDTH, 3 * ATTN_WIDTH + 2 * HGRN_WIDTH,
          3 * ATTN_WIDTH + 3 * HGRN_WIDTH)
Q_BLOCK = 128
CHUNK = 64
NORM_EPS = 1e-6
SUBLN_EPS = 1e-5

kernel_name = 'hymba_diffattn_hgrn2_macaron'


def rmsnorm(x, w, eps=NORM_EPS):
    x32 = x.astype(jnp.float32)
    y = x32 * lax.rsqrt(jnp.mean(x32 * x32, axis=-1, keepdims=True) + eps)
    return (y * w.astype(jnp.float32)).astype(x.dtype)


def swiglu(h, w_gate, w_up, w_down):
    return (jax.nn.silu(h @ w_gate) * (h @ w_up)) @ w_down


def diff_attention(q, k, v, lam):
    B, S, H, _, d = q.shape
    nb = S // Q_BLOCK
    q_blocks = jnp.moveaxis((q * (d ** -0.5)).reshape(B, nb, Q_BLOCK, H, 2, d), 1, 0)
    key_pos = jnp.arange(S)

    def one_block(args):
        blk, qb = args
        q_pos = blk * Q_BLOCK + jnp.arange(Q_BLOCK)
        s = jnp.einsum('bqhcd,bkhcd->bhcqk', qb, k, preferred_element_type=jnp.float32)
        causal = key_pos[None, :] <= q_pos[:, None]
        p = jax.nn.softmax(jnp.where(causal, s, -jnp.inf), axis=-1)
        p_diff = p[:, :, 0] - lam * p[:, :, 1]
        return jnp.einsum('bhqk,bkhv->bqhv', p_diff.astype(v.dtype), v)

    o = lax.map(one_block, (jnp.arange(nb), q_blocks))
    return jnp.moveaxis(o, 0, 1).reshape(B, S, H, v.shape[-1])


def hgrn2_chunked(q, z_f, i, lb):
    f32 = jnp.float32
    q = q.astype(f32)
    z = z_f.astype(f32)
    v = i.astype(f32)
    lb = lb.astype(f32)
    log_f = jnp.logaddexp(jnp.log(lb), jnp.log1p(-lb) + jax.nn.log_sigmoid(z))
    k = (1.0 - lb) * jax.nn.sigmoid(-z)
    B, S, H, dk = q.shape
    dv = v.shape[-1]
    n = S // CHUNK

    def to_chunks(t):
        return t.reshape(B, n, CHUNK, H, t.shape[-1]).transpose(1, 0, 3, 2, 4)

    mask = jnp.tril(jnp.ones((CHUNK, CHUNK), dtype=bool))

    def step(state, inp):
        qc, gc, kc, vc = inp
        b = jnp.cumsum(gc, axis=2)
        o_inter = jnp.einsum('bhtk,bhkv->bhtv', qc * jnp.exp(b), state)
        rel = b[:, :, :, None, :] - b[:, :, None, :, :]
        decay = jnp.exp(jnp.where(mask[:, :, None], rel, -jnp.inf))
        scores = jnp.einsum('bhtk,bhtsk,bhsk->bhts', qc, decay, kc)
        o_intra = jnp.einsum('bhts,bhsv->bhtv', scores, vc)
        b_last = b[:, :, -1:, :]
        state = (jnp.exp(b_last[:, :, 0, :])[..., None] * state
                 + jnp.einsum('bhsk,bhsv->bhkv', kc * jnp.exp(b_last - b), vc))
        return state, o_inter + o_intra

    state0 = jnp.zeros((B, H, dk, dv), f32)
    _, o = lax.scan(step, state0, (to_chunks(q), to_chunks(log_f), to_chunks(k), to_chunks(v)))
    return o.transpose(1, 0, 3, 2, 4).reshape(B, S, H, dv)


def hybrid_mixer(h, w_in, lq1, lk1, lq2, lk2, subln_w, lb, gnorm_w, w_out, layer_idx):
    B, S, _ = h.shape
    proj = h @ w_in
    q_a, k_a, v_a, q_h, f_h, i_h, g_h = jnp.split(proj, SPLITS, axis=-1)

    lambda_init = 0.8 - 0.6 * math.exp(-0.3 * layer_idx)
    lam = (jnp.exp(jnp.sum(lq1.astype(jnp.float32) * lk1.astype(jnp.float32)))
           - jnp.exp(jnp.sum(lq2.astype(jnp.float32) * lk2.astype(jnp.float32)))
           + lambda_init)
    attn = diff_attention(q_a.reshape(B, S, ATTN_HEADS, 2, ATTN_HEAD_DIM),
                          k_a.reshape(B, S, ATTN_HEADS, 2, ATTN_HEAD_DIM),
                          v_a.reshape(B, S, ATTN_HEADS, ATTN_V_DIM), lam)
    attn = (rmsnorm(attn, subln_w, SUBLN_EPS) * (1.0 - lambda_init)).reshape(B, S, ATTN_WIDTH)

    rec = hgrn2_chunked(q_h.reshape(B, S, HGRN_HEADS, HGRN_KEY_DIM),
                        f_h.reshape(B, S, HGRN_HEADS, HGRN_KEY_DIM),
                        i_h.reshape(B, S, HGRN_HEADS, HGRN_VAL_DIM),
                        lb.reshape(HGRN_HEADS, HGRN_KEY_DIM))
    rec = rmsnorm(rec.reshape(B, S, HGRN_WIDTH).astype(h.dtype), gnorm_w) * jax.nn.silu(g_h)

    return jnp.concatenate([attn.astype(h.dtype), rec], axis=-1) @ w_out


def setup_inputs(seed: int = 0) -> dict:
    key = jax.random.key(seed)
    ks = jax.random.split(key, 20)
    f32 = jnp.float32

    def nrm(k, shape, scale):
        return jax.random.normal(k, shape, f32) * scale

    def gain(k, shape):
        return 1.0 + 0.02 * jax.random.normal(k, shape, f32)

    L, D, F = DEPTH, D_MODEL, D_FF
    return {
        'x': nrm(ks[0], (BATCH, SEQ, D), 1.0),
        'ffn1_norm': gain(ks[1], (L, D)),
        'ffn1_w_gate': nrm(ks[2], (L, D, F), D ** -0.5),
        'ffn1_w_up': nrm(ks[3], (L, D, F), D ** -0.5),
        'ffn1_w_down': nrm(ks[4], (L, F, D), F ** -0.5),
        'mix_norm': gain(ks[5], (L, D)),
        'w_in': nrm(ks[6], (L, D, IN_WIDTH), D ** -0.5),
        'lambda_q1': nrm(ks[7], (L, ATTN_HEAD_DIM), 0.1),
        'lambda_k1': nrm(ks[8], (L, ATTN_HEAD_DIM), 0.1),
        'lambda_q2': nrm(ks[9], (L, ATTN_HEAD_DIM), 0.1),
        'lambda_k2': nrm(ks[10], (L, ATTN_HEAD_DIM), 0.1),
        'attn_subln': gain(ks[11], (L, ATTN_V_DIM)),
        'hgrn_lower_bounds': nrm(ks[12], (L, HGRN_HEADS * HGRN_KEY_DIM), 0.5),
        'hgrn_out_norm': gain(ks[13], (L, HGRN_WIDTH)),
        'w_out': nrm(ks[14], (L, D, D), D ** -0.5),
        'ffn2_norm': gain(ks[15], (L, D)),
        'ffn2_w_gate': nrm(ks[16], (L, D, F), D ** -0.5),
        'ffn2_w_up': nrm(ks[17], (L, D, F), D ** -0.5),
        'ffn2_w_down': nrm(ks[18], (L, F, D), F ** -0.5),
        'final_norm': gain(ks[19], (D,)),
    }


def reference(x, ffn1_norm, ffn1_w_gate, ffn1_w_up, ffn1_w_down, mix_norm, w_in,
              lambda_q1, lambda_k1, lambda_q2, lambda_k2, attn_subln, hgrn_lower_bounds,
              hgrn_out_norm, w_out, ffn2_norm, ffn2_w_gate, ffn2_w_up, ffn2_w_down,
              final_norm):
    lb_all = jnp.cumsum(jax.nn.softmax(hgrn_lower_bounds.astype(jnp.float32), axis=0), axis=0)
    lb_all = jnp.clip(lb_all - lb_all[0:1], 0.0, 1.0)
    for l in range(DEPTH):
        x = x + 0.5 * swiglu(rmsnorm(x, ffn1_norm[l]), ffn1_w_gate[l], ffn1_w_up[l], ffn1_w_down[l])
        x = x + hybrid_mixer(rmsnorm(x, mix_norm[l]), w_in[l], lambda_q1[l], lambda_k1[l],
                             lambda_q2[l], lambda_k2[l], attn_subln[l], lb_all[l],
                             hgrn_out_norm[l], w_out[l], l)
        x = x + 0.5 * swiglu(rmsnorm(x, ffn2_norm[l]), ffn2_w_gate[l], ffn2_w_up[l], ffn2_w_down[l])
    return rmsnorm(x, final_norm)
```

```python
import math
from functools import partial

import numpy as np
import jax
import jax.numpy as jnp
from jax import lax
from jax.experimental import pallas as pl
from jax.experimental.pallas import tpu as pltpu

D_MODEL = 2048
DEPTH = 2
ATTN_WIDTH = D_MODEL // 2
HGRN_WIDTH = D_MODEL - ATTN_WIDTH
ATTN_HEADS = 8
ATTN_HEAD_DIM = ATTN_WIDTH // (2 * ATTN_HEADS)
ATTN_V_DIM = 2 * ATTN_HEAD_DIM
HGRN_HEADS = 8
HGRN_KEY_DIM = 128
HGRN_VAL_DIM = 128
D_FF = int(math.ceil(8 * D_MODEL / 3 / 256)) * 256
NORM_EPS = 1e-6
SUBLN_EPS = 1e-5

LANES = 128
V7X_VMEM_BYTES = 64 * 1024 * 1024
VMEM_LIMIT = V7X_VMEM_BYTES - 10 * 1024 * 1024

FFN_TM = 512
FFN_TF = 512
PROJ_TM = 1024
PROJ_TN = 512
OUT_TM = 512
ATTN_TQ = 256
ATTN_TK = 512
HGRN_CHUNK = 256

NT_DIMS = (((1,), (1,)), ((), ()))
TN_DIMS = (((0,), (0,)), ((), ()))


def _rms_scale(x, eps):
    return lax.rsqrt(jnp.mean(x * x, axis=-1, keepdims=True) + eps)


def _ffn_kernel(*refs, final):
    if final:
        x_ref, nw_ref, wg_ref, wu_ref, wd_ref, fw_ref, o_ref, h_sc, acc_sc = refs
    else:
        x_ref, nw_ref, wg_ref, wu_ref, wd_ref, o_ref, h_sc, acc_sc = refs
    j = pl.program_id(1)

    @pl.when(j == 0)
    def _():
        x = x_ref[...]
        h_sc[...] = (x * _rms_scale(x, NORM_EPS) * nw_ref[...]).astype(jnp.bfloat16)
        acc_sc[...] = jnp.zeros_like(acc_sc)

    h = h_sc[...]
    g = jnp.dot(h, wg_ref[...], preferred_element_type=jnp.float32)
    u = jnp.dot(h, wu_ref[...], preferred_element_type=jnp.float32)
    a = (g * jax.nn.sigmoid(g) * u).astype(jnp.bfloat16)
    acc_sc[...] += jnp.dot(a, wd_ref[...], preferred_element_type=jnp.float32)

    @pl.when(j == pl.num_programs(1) - 1)
    def _():
        y = x_ref[...] + 0.5 * acc_sc[...]
        if final:
            y = y * _rms_scale(y, NORM_EPS) * fw_ref[...]
        o_ref[...] = y


def _ffn(x, norm_w, wg, wu, wd, final_w=None):
    m, d = x.shape
    f = wg.shape[1]
    final = final_w is not None
    in_specs = [
        pl.BlockSpec((FFN_TM, d), lambda i, j: (i, 0)),
        pl.BlockSpec((1, d), lambda i, j: (0, 0)),
        pl.BlockSpec((d, FFN_TF), lambda i, j: (0, j)),
        pl.BlockSpec((d, FFN_TF), lambda i, j: (0, j)),
        pl.BlockSpec((FFN_TF, d), lambda i, j: (j, 0)),
    ]
    args = [x, norm_w.reshape(1, d), wg, wu, wd]
    if final:
        in_specs.append(pl.BlockSpec((1, d), lambda i, j: (0, 0)))
        args.append(final_w.reshape(1, d))
    return pl.pallas_call(
        partial(_ffn_kernel, final=final),
        grid=(m // FFN_TM, f // FFN_TF),
        in_specs=in_specs,
        out_specs=pl.BlockSpec((FFN_TM, d), lambda i, j: (i, 0)),
        out_shape=jax.ShapeDtypeStruct((m, d), jnp.float32),
        scratch_shapes=[pltpu.VMEM((FFN_TM, d), jnp.bfloat16),
                        pltpu.VMEM((FFN_TM, d), jnp.float32)],
        compiler_params=pltpu.CompilerParams(
            dimension_semantics=("parallel", "arbitrary"), vmem_limit_bytes=VMEM_LIMIT),
        name="ffn_final" if final else "ffn",
    )(*args)


def _proj_kernel(x_ref, nw_ref, w_ref, oa_ref, oh_ref, h_sc, *, n_attn_blocks):
    j = pl.program_id(1)

    @pl.when(j == 0)
    def _():
        x = x_ref[...]
        h_sc[...] = (x * _rms_scale(x, NORM_EPS) * nw_ref[...]).astype(jnp.bfloat16)

    y = jnp.dot(h_sc[...], w_ref[...], preferred_element_type=jnp.float32)

    @pl.when(j < n_attn_blocks)
    def _():
        oa_ref[...] = y.astype(jnp.bfloat16)

    @pl.when(j >= n_attn_blocks)
    def _():
        oh_ref[...] = y


def _proj(x, norm_w, w_in):
    m, d = x.shape
    n = w_in.shape[1]
    n_attn = 3 * ATTN_WIDTH
    na = n_attn // PROJ_TN
    return pl.pallas_call(
        partial(_proj_kernel, n_attn_blocks=na),
        grid=(m // PROJ_TM, n // PROJ_TN),
        in_specs=[
            pl.BlockSpec((PROJ_TM, d), lambda i, j: (i, 0)),
            pl.BlockSpec((1, d), lambda i, j: (0, 0)),
            pl.BlockSpec((d, PROJ_TN), lambda i, j: (0, j)),
        ],
        out_specs=[
            pl.BlockSpec((PROJ_TM, PROJ_TN), lambda i, j: (i, jnp.minimum(j, na - 1))),
            pl.BlockSpec((PROJ_TM, PROJ_TN), lambda i, j: (i, jnp.maximum(j - na, 0))),
        ],
        out_shape=[jax.ShapeDtypeStruct((m, n_attn), jnp.bfloat16),
                   jax.ShapeDtypeStruct((m, n - n_attn), jnp.float32)],
        scratch_shapes=[pltpu.VMEM((PROJ_TM, d), jnp.bfloat16)],
        compiler_params=pltpu.CompilerParams(
            dimension_semantics=("parallel", "arbitrary"), vmem_limit_bytes=VMEM_LIMIT),
        name="in_proj",
    )(x, norm_w.reshape(1, d), w_in)


def _attn_kernel(lq1_ref, lk1_ref, lq2_ref, lk2_ref, sw_ref, q_ref, k_ref, v_ref, o_ref,
                 qq_sc, m_sc, l_sc, acc_sc, *, lambda_init):
    tq, tk = ATTN_TQ, ATTN_TK
    qi = pl.program_id(2)

    q = q_ref[0]
    lane = lax.broadcasted_iota(jnp.int32, q.shape, 1)
    qs = q * jnp.asarray(ATTN_HEAD_DIM ** -0.5, q.dtype)
    qq_sc[:tq, :] = jnp.where(lane < ATTN_HEAD_DIM, qs, jnp.zeros_like(qs))
    qq_sc[tq:, :] = jnp.where(lane >= ATTN_HEAD_DIM, qs, jnp.zeros_like(qs))
    m_sc[...] = jnp.full_like(m_sc, -jnp.inf)
    l_sc[...] = jnp.zeros_like(l_sc)
    acc_sc[...] = jnp.zeros_like(acc_sc)

    def step(kb, masked):
        start = pl.multiple_of(kb * tk, tk)
        k = k_ref[0, pl.ds(start, tk), :]
        v = v_ref[0, pl.ds(start, tk), :]
        s = lax.dot_general(qq_sc[...], k, NT_DIMS, preferred_element_type=jnp.float32)
        if masked:
            row = lax.broadcasted_iota(jnp.int32, (tq, tk), 0) + qi * tq
            col = lax.broadcasted_iota(jnp.int32, (tq, tk), 1) + kb * tk
            keep = col <= row
            keep = jnp.concatenate([keep, keep], axis=0)
            s = jnp.where(keep, s, -jnp.inf)
        m_prev = m_sc[...]
        m_new = jnp.maximum(m_prev, jnp.max(s, axis=-1, keepdims=True))
        alpha = jnp.exp(m_prev - m_new)
        p = jnp.exp(s - m_new)
        l_sc[...] = alpha * l_sc[...] + jnp.sum(p, axis=-1, keepdims=True)
        acc_sc[...] = alpha * acc_sc[...] + jnp.dot(
            p.astype(jnp.bfloat16), v, preferred_element_type=jnp.float32)
        m_sc[...] = m_new

    diag = (qi * tq) // tk

    def body(kb, carry):
        step(kb, masked=False)
        return carry

    lax.fori_loop(0, diag, body, 0)
    step(diag, masked=True)

    lam = (jnp.exp(jnp.sum(lq1_ref[...] * lk1_ref[...], axis=-1, keepdims=True))
           - jnp.exp(jnp.sum(lq2_ref[...] * lk2_ref[...], axis=-1, keepdims=True))
           + lambda_init)
    acc = acc_sc[...]
    l = l_sc[...]
    o = acc[:tq] / l[:tq] - lam * (acc[tq:] / l[tq:])
    y = o * _rms_scale(o, SUBLN_EPS) * sw_ref[...] * (1.0 - lambda_init)
    o_ref[0] = y.astype(o_ref.dtype)


def _attention(qkv, lq1, lk1, lq2, lk2, subln_w, lambda_init):
    b, s, _ = qkv.shape
    hd = ATTN_V_DIM
    nh = ATTN_HEADS
    small = lambda n: pl.BlockSpec((1, n), lambda bi, hi, qi: (0, 0))
    return pl.pallas_call(
        partial(_attn_kernel, lambda_init=lambda_init),
        grid=(b, nh, s // ATTN_TQ),
        in_specs=[
            small(ATTN_HEAD_DIM), small(ATTN_HEAD_DIM), small(ATTN_HEAD_DIM), small(ATTN_HEAD_DIM),
            small(hd),
            pl.BlockSpec((1, ATTN_TQ, hd), lambda bi, hi, qi: (bi, qi, hi)),
            pl.BlockSpec((1, s, hd), lambda bi, hi, qi: (bi, 0, nh + hi)),
            pl.BlockSpec((1, s, hd), lambda bi, hi, qi: (bi, 0, 2 * nh + hi)),
        ],
        out_specs=pl.BlockSpec((1, ATTN_TQ, hd), lambda bi, hi, qi: (bi, qi, hi)),
        out_shape=jax.ShapeDtypeStruct((b, s, ATTN_WIDTH), jnp.bfloat16),
        scratch_shapes=[pltpu.VMEM((2 * ATTN_TQ, hd), jnp.bfloat16),
                        pltpu.VMEM((2 * ATTN_TQ, 1), jnp.float32),
                        pltpu.VMEM((2 * ATTN_TQ, 1), jnp.float32),
                        pltpu.VMEM((2 * ATTN_TQ, hd), jnp.float32)],
        compiler_params=pltpu.CompilerParams(
            dimension_semantics=("parallel", "parallel", "arbitrary"),
            vmem_limit_bytes=VMEM_LIMIT),
        name="diff_attn",
    )(lq1.reshape(1, -1), lk1.reshape(1, -1), lq2.reshape(1, -1), lk2.reshape(1, -1),
      subln_w.reshape(1, -1), qkv, qkv, qkv)


def _hgrn_levels(chunk):
    levels, size = [], chunk
    while size >= 2:
        levels.append(size)
        size //= 2
    return levels


def _hgrn_sum_matrix(chunk):
    x = np.arange(chunk)[:, None]
    u = np.arange(chunk)[None, :]
    secs = [u <= x, u > x]
    for size in _hgrn_levels(chunk):
        mid = (x // size) * size + size // 2 - 1
        secs.append(np.where(x > mid, (u > mid) & (u <= x), (u > x) & (u <= mid)))
    return np.concatenate(secs, axis=0).astype(np.float32)


def _hgrn_kernel(w_ref, lb_ref, gw_ref, q_ref, z_ref, i_ref, g_ref, o_ref, st_sc, oall_sc):
    c = HGRN_CHUNK
    t = pl.program_id(1)
    h = pl.program_id(2)
    bf16, f32 = jnp.bfloat16, jnp.float32

    @pl.when(t == 0)
    def _():
        st_sc[h] = jnp.zeros((HGRN_VAL_DIM, HGRN_KEY_DIM), f32)

    q = q_ref[0]
    z = z_ref[0]
    v = i_ref[0].astype(bf16)
    lb = lb_ref[...]

    log_sig = jnp.minimum(z, 0.0) - jnp.log1p(jnp.exp(-jnp.abs(z)))
    a0 = jnp.log(lb)
    a1 = jnp.log1p(-lb) + log_sig
    log_f = jnp.maximum(a0, a1) + jnp.log1p(jnp.exp(-jnp.abs(a0 - a1)))
    kk = (1.0 - lb) * jax.nn.sigmoid(-z)

    hi = log_f.astype(bf16)
    r1 = log_f - hi.astype(f32)
    mid = r1.astype(bf16)
    lo = (r1 - mid.astype(f32)).astype(bf16)
    parts = jnp.concatenate([hi, mid, lo], axis=1)
    sums = jnp.dot(w_ref[...], parts, preferred_element_type=f32)
    expo = (sums[:, :LANES] + sums[:, LANES:2 * LANES]) + sums[:, 2 * LANES:]
    decay = jnp.exp(expo)

    d_cum = decay[:c]
    d_rev = decay[c:2 * c]
    d_last = decay[c - 1:c]

    row = lax.broadcasted_iota(jnp.int32, (c, c), 0)
    col = lax.broadcasted_iota(jnp.int32, (c, c), 1)
    rid = lax.broadcasted_iota(jnp.int32, (c, 1), 0)

    scores = jnp.where(row == col,
                       lax.dot_general(q.astype(bf16), kk.astype(bf16), NT_DIMS,
                                       preferred_element_type=f32), 0.0)
    for n, size in enumerate(_hgrn_levels(c)):
        d = decay[(2 + n) * c:(3 + n) * c]
        lower = (rid & (size - 1)) >= size // 2
        qn = jnp.where(lower, q * d, 0.0).astype(bf16)
        kn = jnp.where(lower, 0.0, kk * d).astype(bf16)
        pn = lax.dot_general(qn, kn, NT_DIMS, preferred_element_type=f32)
        scores = scores + jnp.where((row ^ col) < size, pn, 0.0)

    st = st_sc[h]
    o = lax.dot_general((q * d_cum).astype(bf16), st.astype(bf16), NT_DIMS,
                        preferred_element_type=f32)
    o = o + jnp.dot(scores.astype(bf16), v, preferred_element_type=f32)
    st_sc[h] = st * d_last + lax.dot_general(v, (kk * d_rev).astype(bf16), TN_DIMS,
                                             preferred_element_type=f32)
    oall_sc[h] = o

    @pl.when(h == HGRN_HEADS - 1)
    def _():
        ssq = jnp.zeros((c, 1), f32)
        for hh in range(HGRN_HEADS):
            oh = oall_sc[hh]
            ssq = ssq + jnp.sum(oh * oh, axis=-1, keepdims=True)
        inv = lax.rsqrt(ssq * (1.0 / HGRN_WIDTH) + NORM_EPS)
        for hh in range(HGRN_HEADS):
            sl = slice(hh * HGRN_VAL_DIM, (hh + 1) * HGRN_VAL_DIM)
            g = g_ref[0, :, sl]
            y = oall_sc[hh] * inv * gw_ref[:, sl]
            o_ref[0, :, sl] = (y * (g * jax.nn.sigmoid(g))).astype(o_ref.dtype)


def _hgrn(hg, lb, gnorm_w):
    b, s, _ = hg.shape
    c = HGRN_CHUNK
    nh = HGRN_HEADS
    w = jnp.asarray(_hgrn_sum_matrix(c), dtype=jnp.bfloat16)
    head_in = lambda off: pl.BlockSpec((1, c, LANES), lambda bi, ti, hi: (bi, ti, off * nh + hi))
    return pl.pallas_call(
        _hgrn_kernel,
        grid=(b, s // c, nh),
        in_specs=[
            pl.BlockSpec(w.shape, lambda bi, ti, hi: (0, 0)),
            pl.BlockSpec((1, HGRN_KEY_DIM), lambda bi, ti, hi: (0, hi)),
            pl.BlockSpec((1, HGRN_WIDTH), lambda bi, ti, hi: (0, 0)),
            head_in(0), head_in(1), head_in(2),
            pl.BlockSpec((1, c, HGRN_WIDTH), lambda bi, ti, hi: (bi, ti, 3)),
        ],
        out_specs=pl.BlockSpec((1, c, HGRN_WIDTH), lambda bi, ti, hi: (bi, ti, 0)),
        out_shape=jax.ShapeDtypeStruct((b, s, HGRN_WIDTH), jnp.bfloat16),
        scratch_shapes=[pltpu.VMEM((nh, HGRN_VAL_DIM, HGRN_KEY_DIM), jnp.float32),
                        pltpu.VMEM((nh, c, HGRN_VAL_DIM), jnp.float32)],
        compiler_params=pltpu.CompilerParams(
            dimension_semantics=("parallel", "arbitrary", "arbitrary"),
            vmem_limit_bytes=VMEM_LIMIT),
        name="hgrn2",
    )(w, lb.reshape(1, -1), gnorm_w.reshape(1, -1), hg, hg, hg, hg)


def _outproj_kernel(x_ref, a_ref, r_ref, wa_ref, wr_ref, o_ref):
    y = jnp.dot(a_ref[...], wa_ref[...], preferred_element_type=jnp.float32)
    y = y + jnp.dot(r_ref[...], wr_ref[...], preferred_element_type=jnp.float32)
    o_ref[...] = x_ref[...] + y


def _outproj(x, attn, rec, w_out):
    m, d = x.shape
    ka, kr = attn.shape[1], rec.shape[1]
    return pl.pallas_call(
        _outproj_kernel,
        grid=(m // OUT_TM,),
        in_specs=[
            pl.BlockSpec((OUT_TM, d), lambda i: (i, 0)),
            pl.BlockSpec((OUT_TM, ka), lambda i: (i, 0)),
            pl.BlockSpec((OUT_TM, kr), lambda i: (i, 0)),
            pl.BlockSpec((ka, d), lambda i: (0, 0)),
            pl.BlockSpec((kr, d), lambda i: (1, 0)),
        ],
        out_specs=pl.BlockSpec((OUT_TM, d), lambda i: (i, 0)),
        out_shape=jax.ShapeDtypeStruct((m, d), jnp.float32),
        compiler_params=pltpu.CompilerParams(
            dimension_semantics=("parallel",), vmem_limit_bytes=VMEM_LIMIT),
        name="out_proj",
    )(x, attn, rec, w_out, w_out)


def kernel(x, ffn1_norm, ffn1_w_gate, ffn1_w_up, ffn1_w_down, mix_norm, w_in, lambda_q1, lambda_k1,
           lambda_q2, lambda_k2, attn_subln, hgrn_lower_bounds, hgrn_out_norm, w_out, ffn2_norm,
           ffn2_w_gate, ffn2_w_up, ffn2_w_down, final_norm):
    b, s, d = x.shape
    bf16 = jnp.bfloat16
    lb_all = jnp.cumsum(jax.nn.softmax(hgrn_lower_bounds.astype(jnp.float32), axis=0), axis=0)
    lb_all = jnp.clip(lb_all - lb_all[0:1], 0.0, 1.0)

    xf = x.reshape(b * s, d)
    for l in range(DEPTH):
        lambda_init = 0.8 - 0.6 * math.exp(-0.3 * l)
        xf = _ffn(xf, ffn1_norm[l], ffn1_w_gate[l].astype(bf16), ffn1_w_up[l].astype(bf16),
                  ffn1_w_down[l].astype(bf16))
        qkv, hg = _proj(xf, mix_norm[l], w_in[l].astype(bf16))
        attn = _attention(qkv.reshape(b, s, -1), lambda_q1[l], lambda_k1[l], lambda_q2[l],
                          lambda_k2[l], attn_subln[l], lambda_init)
        rec = _hgrn(hg.reshape(b, s, -1), lb_all[l], hgrn_out_norm[l])
        xf = _outproj(xf, attn.reshape(b * s, -1), rec.reshape(b * s, -1), w_out[l].astype(bf16))
        xf = _ffn(xf, ffn2_norm[l], ffn2_w_gate[l].astype(bf16), ffn2_w_up[l].astype(bf16),
                  ffn2_w_down[l].astype(bf16),
                  final_w=final_norm if l == DEPTH - 1 else None)
    return xf.reshape(b, s, d)
```

```python
import math
from functools import partial

import numpy as np
import jax
import jax.numpy as jnp
from jax import lax
from jax.experimental import pallas as pl
from jax.experimental.pallas import tpu as pltpu

D_MODEL = 2048
DEPTH = 2
ATTN_WIDTH = D_MODEL // 2
HGRN_WIDTH = D_MODEL - ATTN_WIDTH
ATTN_HEADS = 8
ATTN_HEAD_DIM = ATTN_WIDTH // (2 * ATTN_HEADS)
ATTN_V_DIM = 2 * ATTN_HEAD_DIM
HGRN_HEADS = 8
HGRN_KEY_DIM = 128
HGRN_VAL_DIM = 128
D_FF = int(math.ceil(8 * D_MODEL / 3 / 256)) * 256
NORM_EPS = 1e-6
SUBLN_EPS = 1e-5

LANES = 128
V7X_VMEM_BYTES = 64 * 1024 * 1024
VMEM_LIMIT = V7X_VMEM_BYTES - 10 * 1024 * 1024

FFN_TM = 512
FFN_TF = 512
PROJ_TM = 1024
PROJ_TN = 512
OUT_TM = 512
ATTN_TQ = 1024
ATTN_TK = 256
BF16_SUBLANES = 16
ATTN_VT_ROWS = ATTN_V_DIM + BF16_SUBLANES
QUERY_SCALE = ATTN_HEAD_DIM ** -0.5 * math.log2(math.e)
HGRN_CHUNK = 256

NT_DIMS = (((1,), (1,)), ((), ()))
TN_DIMS = (((0,), (0,)), ((), ()))


def _rms_scale(x, eps):
    return lax.rsqrt(jnp.mean(x * x, axis=-1, keepdims=True) + eps)


def _ffn_kernel(*refs, final):
    if final:
        x_ref, nw_ref, wg_ref, wu_ref, wd_ref, fw_ref, o_ref, h_sc, acc_sc = refs
    else:
        x_ref, nw_ref, wg_ref, wu_ref, wd_ref, o_ref, h_sc, acc_sc = refs
    j = pl.program_id(1)

    @pl.when(j == 0)
    def _():
        x = x_ref[...]
        h_sc[...] = (x * _rms_scale(x, NORM_EPS) * nw_ref[...]).astype(jnp.bfloat16)
        acc_sc[...] = jnp.zeros_like(acc_sc)

    h = h_sc[...]
    g = jnp.dot(h, wg_ref[...], preferred_element_type=jnp.float32)
    u = jnp.dot(h, wu_ref[...], preferred_element_type=jnp.float32)
    a = (g * jax.nn.sigmoid(g) * u).astype(jnp.bfloat16)
    acc_sc[...] += jnp.dot(a, wd_ref[...], preferred_element_type=jnp.float32)

    @pl.when(j == pl.num_programs(1) - 1)
    def _():
        y = x_ref[...] + 0.5 * acc_sc[...]
        if final:
            y = y * _rms_scale(y, NORM_EPS) * fw_ref[...]
        o_ref[...] = y


def _ffn(x, norm_w, wg, wu, wd, final_w=None):
    m, d = x.shape
    f = wg.shape[1]
    final = final_w is not None
    in_specs = [
        pl.BlockSpec((FFN_TM, d), lambda i, j: (i, 0)),
        pl.BlockSpec((1, d), lambda i, j: (0, 0)),
        pl.BlockSpec((d, FFN_TF), lambda i, j: (0, j)),
        pl.BlockSpec((d, FFN_TF), lambda i, j: (0, j)),
        pl.BlockSpec((FFN_TF, d), lambda i, j: (j, 0)),
    ]
    args = [x, norm_w.reshape(1, d), wg, wu, wd]
    if final:
        in_specs.append(pl.BlockSpec((1, d), lambda i, j: (0, 0)))
        args.append(final_w.reshape(1, d))
    return pl.pallas_call(
        partial(_ffn_kernel, final=final),
        grid=(m // FFN_TM, f // FFN_TF),
        in_specs=in_specs,
        out_specs=pl.BlockSpec((FFN_TM, d), lambda i, j: (i, 0)),
        out_shape=jax.ShapeDtypeStruct((m, d), jnp.float32),
        scratch_shapes=[pltpu.VMEM((FFN_TM, d), jnp.bfloat16),
                        pltpu.VMEM((FFN_TM, d), jnp.float32)],
        compiler_params=pltpu.CompilerParams(
            dimension_semantics=("parallel", "arbitrary"), vmem_limit_bytes=VMEM_LIMIT),
        name="ffn_final" if final else "ffn",
    )(*args)


def _proj_kernel(x_ref, nw_ref, w_ref, oa_ref, oh_ref, h_sc, *, n_attn_blocks, n_query_blocks):
    j = pl.program_id(1)

    @pl.when(j == 0)
    def _():
        x = x_ref[...]
        h_sc[...] = (x * _rms_scale(x, NORM_EPS) * nw_ref[...]).astype(jnp.bfloat16)

    y = jnp.dot(h_sc[...], w_ref[...], preferred_element_type=jnp.float32)

    @pl.when(j < n_attn_blocks)
    def _():
        scale = jnp.where(j < n_query_blocks, QUERY_SCALE, 1.0)
        oa_ref[...] = (y * scale).astype(jnp.bfloat16)

    @pl.when(j >= n_attn_blocks)
    def _():
        oh_ref[...] = y


def _proj(x, norm_w, w_in):
    m, d = x.shape
    n = w_in.shape[1]
    n_attn = 3 * ATTN_WIDTH
    na = n_attn // PROJ_TN
    return pl.pallas_call(
        partial(_proj_kernel, n_attn_blocks=na, n_query_blocks=ATTN_WIDTH // PROJ_TN),
        grid=(m // PROJ_TM, n // PROJ_TN),
        in_specs=[
            pl.BlockSpec((PROJ_TM, d), lambda i, j: (i, 0)),
            pl.BlockSpec((1, d), lambda i, j: (0, 0)),
            pl.BlockSpec((d, PROJ_TN), lambda i, j: (0, j)),
        ],
        out_specs=[
            pl.BlockSpec((PROJ_TM, PROJ_TN), lambda i, j: (i, jnp.minimum(j, na - 1))),
            pl.BlockSpec((PROJ_TM, PROJ_TN), lambda i, j: (i, jnp.maximum(j - na, 0))),
        ],
        out_shape=[jax.ShapeDtypeStruct((m, n_attn), jnp.bfloat16),
                   jax.ShapeDtypeStruct((m, n - n_attn), jnp.float32)],
        scratch_shapes=[pltpu.VMEM((PROJ_TM, d), jnp.bfloat16)],
        compiler_params=pltpu.CompilerParams(
            dimension_semantics=("parallel", "arbitrary"), vmem_limit_bytes=VMEM_LIMIT),
        name="in_proj",
    )(x, norm_w.reshape(1, d), w_in)


def _attn_kernel(lq1_ref, lk1_ref, lq2_ref, lk2_ref, sw_ref, q_ref, k_ref, v_ref, o_ref,
                 qq_sc, vt_sc, sa_sc, sb_sc, m_sc, acc_sc, *, lambda_init):
    tq, tk = ATTN_TQ, ATTN_TK
    qi = pl.program_id(2)

    @pl.when(qi == 0)
    def _():
        ones = jnp.ones((BF16_SUBLANES, tk), vt_sc.dtype)

        def xpose(c, carry):
            start = pl.multiple_of(c * tk, tk)
            vt_sc[c] = jnp.concatenate([v_ref[0, pl.ds(start, tk), :].T, ones], axis=0)
            return carry
        lax.fori_loop(0, vt_sc.shape[0], xpose, 0)

    q = q_ref[0]
    lane = lax.broadcasted_iota(jnp.int32, q.shape, 1)
    qq_sc[:tq, :] = jnp.where(lane < ATTN_HEAD_DIM, q, jnp.zeros_like(q))
    qq_sc[tq:, :] = jnp.where(lane >= ATTN_HEAD_DIM, q, jnp.zeros_like(q))
    m_sc[...] = jnp.full_like(m_sc, -jnp.inf)
    acc_sc[...] = jnp.zeros_like(acc_sc)

    def scores(kb):
        start = pl.multiple_of(kb * tk, tk)
        return lax.dot_general(k_ref[0, pl.ds(start, tk), :], qq_sc[...], NT_DIMS,
                               preferred_element_type=jnp.float32)

    def consume(s, kb, masked):
        if masked:
            key_pos = lax.broadcasted_iota(jnp.int32, (tk, tq), 0) + kb * tk
            q_pos = lax.broadcasted_iota(jnp.int32, (tk, tq), 1) + qi * tq
            keep = key_pos <= q_pos
            s = jnp.where(jnp.concatenate([keep, keep], axis=1), s, -jnp.inf)
        m_prev = m_sc[...]
        m_new = jnp.maximum(m_prev, jnp.max(s, axis=0, keepdims=True))
        alpha = jnp.exp2(m_prev - m_new)
        p = jnp.exp2(s - m_new)
        acc_sc[...] = alpha * acc_sc[...] + jnp.dot(vt_sc[kb], p.astype(jnp.bfloat16),
                                                    preferred_element_type=jnp.float32)
        m_sc[...] = jnp.maximum(m_new, p[tk - 8:tk - 7, :] - jnp.finfo(jnp.float32).max)

    def group(base, masked, prefetch_next):
        s1 = scores(base + 1)
        sb_sc[...] = scores(base + 2)
        consume(sa_sc[...], base, masked)
        consume(s1, base + 1, masked)
        s3 = scores(base + 3)
        if prefetch_next:
            sa_sc[...] = scores(base + 4)
        consume(sb_sc[...], base + 2, masked)
        consume(s3, base + 3, masked)

    sa_sc[...] = scores(0)

    def body(i, carry):
        group(4 * i, masked=False, prefetch_next=True)
        return carry

    lax.fori_loop(0, qi, body, 0)
    group(4 * qi, masked=True, prefetch_next=False)

    lam = (jnp.exp(jnp.sum(lq1_ref[...] * lk1_ref[...], axis=-1, keepdims=True))
           - jnp.exp(jnp.sum(lq2_ref[...] * lk2_ref[...], axis=-1, keepdims=True))
           + lambda_init)
    acc = acc_sc[...]
    num = acc[:ATTN_V_DIM]
    den = acc[ATTN_V_DIM:ATTN_V_DIM + 1]
    o = num[:, :tq] / den[:, :tq] - lam * (num[:, tq:] / den[:, tq:])
    scale = lax.rsqrt(jnp.mean(o * o, axis=0, keepdims=True) + SUBLN_EPS)
    y = o * scale * sw_ref[...] * (1.0 - lambda_init)
    o_ref[0] = y.T.astype(o_ref.dtype)


def _attention(qkv, lq1, lk1, lq2, lk2, subln_w, lambda_init):
    b, s, _ = qkv.shape
    hd = ATTN_V_DIM
    nh = ATTN_HEADS
    small = lambda n: pl.BlockSpec((1, n), lambda bi, hi, qi: (0, 0))
    return pl.pallas_call(
        partial(_attn_kernel, lambda_init=lambda_init),
        grid=(b, nh, s // ATTN_TQ),
        in_specs=[
            small(ATTN_HEAD_DIM), small(ATTN_HEAD_DIM), small(ATTN_HEAD_DIM), small(ATTN_HEAD_DIM),
            pl.BlockSpec((hd, 1), lambda bi, hi, qi: (0, 0)),
            pl.BlockSpec((1, ATTN_TQ, hd), lambda bi, hi, qi: (bi, qi, hi)),
            pl.BlockSpec((1, s, hd), lambda bi, hi, qi: (bi, 0, nh + hi)),
            pl.BlockSpec((1, s, hd), lambda bi, hi, qi: (bi, 0, 2 * nh + hi)),
        ],
        out_specs=pl.BlockSpec((1, ATTN_TQ, hd), lambda bi, hi, qi: (bi, qi, hi)),
        out_shape=jax.ShapeDtypeStruct((b, s, ATTN_WIDTH), jnp.bfloat16),
        scratch_shapes=[pltpu.VMEM((2 * ATTN_TQ, hd), jnp.bfloat16),
                        pltpu.VMEM((s // ATTN_TK, ATTN_VT_ROWS, ATTN_TK), jnp.bfloat16),
                        pltpu.VMEM((ATTN_TK, 2 * ATTN_TQ), jnp.float32),
                        pltpu.VMEM((ATTN_TK, 2 * ATTN_TQ), jnp.float32),
                        pltpu.VMEM((1, 2 * ATTN_TQ), jnp.float32),
                        pltpu.VMEM((ATTN_VT_ROWS, 2 * ATTN_TQ), jnp.float32)],
        compiler_params=pltpu.CompilerParams(
            dimension_semantics=("parallel", "parallel", "arbitrary"),
            vmem_limit_bytes=VMEM_LIMIT),
        name="diff_attn",
    )(lq1.reshape(1, -1), lk1.reshape(1, -1), lq2.reshape(1, -1), lk2.reshape(1, -1),
      subln_w.reshape(-1, 1), qkv, qkv, qkv)


def _hgrn_levels(chunk):
    levels, size = [], chunk
    while size >= 2:
        levels.append(size)
        size //= 2
    return levels


def _hgrn_sum_matrix(chunk):
    x = np.arange(chunk)[:, None]
    u = np.arange(chunk)[None, :]
    secs = [u <= x, u > x]
    for size in _hgrn_levels(chunk):
        mid = (x // size) * size + size // 2 - 1
        secs.append(np.where(x > mid, (u > mid) & (u <= x), (u > x) & (u <= mid)))
    return np.concatenate(secs, axis=0).astype(np.float32)


def _hgrn_kernel(w_ref, lb_ref, gw_ref, q_ref, z_ref, i_ref, g_ref, o_ref, st_sc, oall_sc):
    c = HGRN_CHUNK
    t = pl.program_id(1)
    h = pl.program_id(2)
    bf16, f32 = jnp.bfloat16, jnp.float32

    @pl.when(t == 0)
    def _():
        st_sc[h] = jnp.zeros((HGRN_VAL_DIM, HGRN_KEY_DIM), f32)

    q = q_ref[0]
    z = z_ref[0]
    v = i_ref[0].astype(bf16)
    lb = lb_ref[...]

    log_sig = jnp.minimum(z, 0.0) - jnp.log1p(jnp.exp(-jnp.abs(z)))
    a0 = jnp.log(lb)
    a1 = jnp.log1p(-lb) + log_sig
    log_f = jnp.maximum(a0, a1) + jnp.log1p(jnp.exp(-jnp.abs(a0 - a1)))
    kk = (1.0 - lb) * jax.nn.sigmoid(-z)

    hi = log_f.astype(bf16)
    r1 = log_f - hi.astype(f32)
    mid = r1.astype(bf16)
    lo = (r1 - mid.astype(f32)).astype(bf16)
    parts = jnp.concatenate([hi, mid, lo], axis=1)
    sums = jnp.dot(w_ref[...], parts, preferred_element_type=f32)
    expo = (sums[:, :LANES] + sums[:, LANES:2 * LANES]) + sums[:, 2 * LANES:]
    decay = jnp.exp(expo)

    d_cum = decay[:c]
    d_rev = decay[c:2 * c]
    d_last = decay[c - 1:c]

    row = lax.broadcasted_iota(jnp.int32, (c, c), 0)
    col = lax.broadcasted_iota(jnp.int32, (c, c), 1)
    rid = lax.broadcasted_iota(jnp.int32, (c, 1), 0)

    scores = jnp.where(row == col,
                       lax.dot_general(q.astype(bf16), kk.astype(bf16), NT_DIMS,
                                       preferred_element_type=f32), 0.0)
    for n, size in enumerate(_hgrn_levels(c)):
        d = decay[(2 + n) * c:(3 + n) * c]
        lower = (rid & (size - 1)) >= size // 2
        qn = jnp.where(lower, q * d, 0.0).astype(bf16)
        kn = jnp.where(lower, 0.0, kk * d).astype(bf16)
        pn = lax.dot_general(qn, kn, NT_DIMS, preferred_element_type=f32)
        scores = scores + jnp.where((row ^ col) < size, pn, 0.0)

    st = st_sc[h]
    o = lax.dot_general((q * d_cum).astype(bf16), st.astype(bf16), NT_DIMS,
                        preferred_element_type=f32)
    o = o + jnp.dot(scores.astype(bf16), v, preferred_element_type=f32)
    st_sc[h] = st * d_last + lax.dot_general(v, (kk * d_rev).astype(bf16), TN_DIMS,
                                             preferred_element_type=f32)
    oall_sc[h] = o

    @pl.when(h == HGRN_HEADS - 1)
    def _():
        ssq = jnp.zeros((c, 1), f32)
        for hh in range(HGRN_HEADS):
            oh = oall_sc[hh]
            ssq = ssq + jnp.sum(oh * oh, axis=-1, keepdims=True)
        inv = lax.rsqrt(ssq * (1.0 / HGRN_WIDTH) + NORM_EPS)
        for hh in range(HGRN_HEADS):
            sl = slice(hh * HGRN_VAL_DIM, (hh + 1) * HGRN_VAL_DIM)
            g = g_ref[0, :, sl]
            y = oall_sc[hh] * inv * gw_ref[:, sl]
            o_ref[0, :, sl] = (y * (g * jax.nn.sigmoid(g))).astype(o_ref.dtype)


def _hgrn(hg, lb, gnorm_w):
    b, s, _ = hg.shape
    c = HGRN_CHUNK
    nh = HGRN_HEADS
    w = jnp.asarray(_hgrn_sum_matrix(c), dtype=jnp.bfloat16)
    head_in = lambda off: pl.BlockSpec((1, c, LANES), lambda bi, ti, hi: (bi, ti, off * nh + hi))
    return pl.pallas_call(
        _hgrn_kernel,
        grid=(b, s // c, nh),
        in_specs=[
            pl.BlockSpec(w.shape, lambda bi, ti, hi: (0, 0)),
            pl.BlockSpec((1, HGRN_KEY_DIM), lambda bi, ti, hi: (0, hi)),
            pl.BlockSpec((1, HGRN_WIDTH), lambda bi, ti, hi: (0, 0)),
            head_in(0), head_in(1), head_in(2),
            pl.BlockSpec((1, c, HGRN_WIDTH), lambda bi, ti, hi: (bi, ti, 3)),
        ],
        out_specs=pl.BlockSpec((1, c, HGRN_WIDTH), lambda bi, ti, hi: (bi, ti, 0)),
        out_shape=jax.ShapeDtypeStruct((b, s, HGRN_WIDTH), jnp.bfloat16),
        scratch_shapes=[pltpu.VMEM((nh, HGRN_VAL_DIM, HGRN_KEY_DIM), jnp.float32),
                        pltpu.VMEM((nh, c, HGRN_VAL_DIM), jnp.float32)],
        compiler_params=pltpu.CompilerParams(
            dimension_semantics=("parallel", "arbitrary", "arbitrary"),
            vmem_limit_bytes=VMEM_LIMIT),
        name="hgrn2",
    )(w, lb.reshape(1, -1), gnorm_w.reshape(1, -1), hg, hg, hg, hg)


def _outproj_kernel(x_ref, a_ref, r_ref, wa_ref, wr_ref, o_ref):
    y = jnp.dot(a_ref[...], wa_ref[...], preferred_element_type=jnp.float32)
    y = y + jnp.dot(r_ref[...], wr_ref[...], preferred_element_type=jnp.float32)
    o_ref[...] = x_ref[...] + y


def _outproj(x, attn, rec, w_out):
    m, d = x.shape
    ka, kr = attn.shape[1], rec.shape[1]
    return pl.pallas_call(
        _outproj_kernel,
        grid=(m // OUT_TM,),
        in_specs=[
            pl.BlockSpec((OUT_TM, d), lambda i: (i, 0)),
            pl.BlockSpec((OUT_TM, ka), lambda i: (i, 0)),
            pl.BlockSpec((OUT_TM, kr), lambda i: (i, 0)),
            pl.BlockSpec((ka, d), lambda i: (0, 0)),
            pl.BlockSpec((kr, d), lambda i: (1, 0)),
        ],
        out_specs=pl.BlockSpec((OUT_TM, d), lambda i: (i, 0)),
        out_shape=jax.ShapeDtypeStruct((m, d), jnp.float32),
        compiler_params=pltpu.CompilerParams(
            dimension_semantics=("parallel",), vmem_limit_bytes=VMEM_LIMIT),
        name="out_proj",
    )(x, attn, rec, w_out, w_out)


def kernel(x, ffn1_norm, ffn1_w_gate, ffn1_w_up, ffn1_w_down, mix_norm, w_in, lambda_q1, lambda_k1,
           lambda_q2, lambda_k2, attn_subln, hgrn_lower_bounds, hgrn_out_norm, w_out, ffn2_norm,
           ffn2_w_gate, ffn2_w_up, ffn2_w_down, final_norm):
    b, s, d = x.shape
    bf16 = jnp.bfloat16
    lb_all = jnp.cumsum(jax.nn.softmax(hgrn_lower_bounds.astype(jnp.float32), axis=0), axis=0)
    lb_all = jnp.clip(lb_all - lb_all[0:1], 0.0, 1.0)

    xf = x.reshape(b * s, d)
    for l in range(DEPTH):
        lambda_init = 0.8 - 0.6 * math.exp(-0.3 * l)
        xf = _ffn(xf, ffn1_norm[l], ffn1_w_gate[l].astype(bf16), ffn1_w_up[l].astype(bf16),
                  ffn1_w_down[l].astype(bf16))
        qkv, hg = _proj(xf, mix_norm[l], w_in[l].astype(bf16))
        attn = _attention(qkv.reshape(b, s, -1), lambda_q1[l], lambda_k1[l], lambda_q2[l],
                          lambda_k2[l], attn_subln[l], lambda_init)
        rec = _hgrn(hg.reshape(b, s, -1), lb_all[l], hgrn_out_norm[l])
        xf = _outproj(xf, attn.reshape(b * s, -1), rec.reshape(b * s, -1), w_out[l].astype(bf16))
        xf = _ffn(xf, ffn2_norm[l], ffn2_w_gate[l].astype(bf16), ffn2_w_up[l].astype(bf16),
                  ffn2_w_down[l].astype(bf16),
                  final_w=final_norm if l == DEPTH - 1 else None)
    return xf.reshape(b, s, d)
```

```python
import math
from functools import partial

import numpy as np
import jax
import jax.numpy as jnp
from jax import lax
from jax.experimental import pallas as pl
from jax.experimental.pallas import tpu as pltpu

D_MODEL = 2048
DEPTH = 2
ATTN_WIDTH = D_MODEL // 2
HGRN_WIDTH = D_MODEL - ATTN_WIDTH
ATTN_HEADS = 8
ATTN_HEAD_DIM = ATTN_WIDTH // (2 * ATTN_HEADS)
ATTN_V_DIM = 2 * ATTN_HEAD_DIM
HGRN_HEADS = 8
HGRN_KEY_DIM = 128
HGRN_VAL_DIM = 128
D_FF = int(math.ceil(8 * D_MODEL / 3 / 256)) * 256
NORM_EPS = 1e-6
SUBLN_EPS = 1e-5

LANES = 128
V7X_VMEM_BYTES = 64 * 1024 * 1024
VMEM_LIMIT = V7X_VMEM_BYTES - 10 * 1024 * 1024

FFN_TM = 512
FFN_TF = 512
PROJ_TM = 1024
PROJ_TN = 1024
CAST_BLOCK_BYTES = 8 * 1024 * 1024
OUT_TM = 512
ATTN_TQ = 1024
ATTN_TK = 256
V7X_MXU_WIDTH = 256
ATTN_LANE_CHUNK = V7X_MXU_WIDTH
BF16_SUBLANES = 16
ATTN_VT_ROWS = ATTN_V_DIM + BF16_SUBLANES
QUERY_SCALE = ATTN_HEAD_DIM ** -0.5 * math.log2(math.e)
HGRN_CHUNK = 256
HGRN_HEADS_PER_STEP = 8

NT_DIMS = (((1,), (1,)), ((), ()))
TN_DIMS = (((0,), (0,)), ((), ()))


def _rms_scale(x, eps):
    return lax.rsqrt(jnp.mean(x * x, axis=-1, keepdims=True) + eps)


def _cast_kernel(w_ref, o_ref):
    o_ref[...] = w_ref[...].astype(o_ref.dtype)


def _cast_bf16(w):
    nl, r, c = w.shape
    rows = CAST_BLOCK_BYTES // (c * w.dtype.itemsize)
    rows = min(r & -r, 1 << (rows.bit_length() - 1))
    return pl.pallas_call(
        _cast_kernel,
        grid=(nl, r // rows),
        in_specs=[pl.BlockSpec((1, rows, c), lambda l, i: (l, i, 0))],
        out_specs=pl.BlockSpec((1, rows, c), lambda l, i: (l, i, 0)),
        out_shape=jax.ShapeDtypeStruct(w.shape, jnp.bfloat16),
        compiler_params=pltpu.CompilerParams(
            dimension_semantics=("parallel", "parallel"), vmem_limit_bytes=VMEM_LIMIT),
        name="cast_bf16",
    )(w)


def _ffn_kernel(*refs, final):
    if final:
        x_ref, nw_ref, wg_ref, wu_ref, wd_ref, fw_ref, o_ref, h_sc, acc_sc = refs
    else:
        x_ref, nw_ref, wg_ref, wu_ref, wd_ref, o_ref, h_sc, acc_sc = refs
    j = pl.program_id(1)

    @pl.when(j == 0)
    def _():
        x = x_ref[...]
        h_sc[...] = (x * _rms_scale(x, NORM_EPS) * nw_ref[...]).astype(jnp.bfloat16)
        acc_sc[...] = jnp.zeros_like(acc_sc)

    h = h_sc[...]
    g = jnp.dot(h, wg_ref[...], preferred_element_type=jnp.float32)
    u = jnp.dot(h, wu_ref[...], preferred_element_type=jnp.float32)
    a = (g * jax.nn.sigmoid(g) * u).astype(jnp.bfloat16)
    acc_sc[...] += jnp.dot(a, wd_ref[...], preferred_element_type=jnp.float32)

    @pl.when(j == pl.num_programs(1) - 1)
    def _():
        y = x_ref[...] + 0.5 * acc_sc[...]
        if final:
            y = y * _rms_scale(y, NORM_EPS) * fw_ref[...]
        o_ref[...] = y


def _ffn(x, norm_w, wg, wu, wd, layer, final_w=None):
    m, d = x.shape
    f = wg.shape[2]
    final = final_w is not None
    in_specs = [
        pl.BlockSpec((FFN_TM, d), lambda i, j: (i, 0)),
        pl.BlockSpec((1, d), lambda i, j: (0, 0)),
        pl.BlockSpec((None, d, FFN_TF), lambda i, j: (layer, 0, j)),
        pl.BlockSpec((None, d, FFN_TF), lambda i, j: (layer, 0, j)),
        pl.BlockSpec((None, FFN_TF, d), lambda i, j: (layer, j, 0)),
    ]
    args = [x, norm_w.reshape(1, d), wg, wu, wd]
    if final:
        in_specs.append(pl.BlockSpec((1, d), lambda i, j: (0, 0)))
        args.append(final_w.reshape(1, d))
    return pl.pallas_call(
        partial(_ffn_kernel, final=final),
        grid=(m // FFN_TM, f // FFN_TF),
        in_specs=in_specs,
        out_specs=pl.BlockSpec((FFN_TM, d), lambda i, j: (i, 0)),
        out_shape=jax.ShapeDtypeStruct((m, d), jnp.float32),
        scratch_shapes=[pltpu.VMEM((FFN_TM, d), jnp.bfloat16),
                        pltpu.VMEM((FFN_TM, d), jnp.float32)],
        compiler_params=pltpu.CompilerParams(
            dimension_semantics=("parallel", "arbitrary"), vmem_limit_bytes=VMEM_LIMIT),
        name="ffn_final" if final else "ffn",
    )(*args)


def _proj_kernel(x_ref, nw_ref, w_ref, oa_ref, oh_ref, h_sc, *, n_attn_blocks, n_query_blocks):
    j = pl.program_id(1)

    @pl.when(j == 0)
    def _():
        x = x_ref[...]
        h_sc[...] = (x * _rms_scale(x, NORM_EPS) * nw_ref[...]).astype(jnp.bfloat16)

    y = jnp.dot(h_sc[...], w_ref[...], preferred_element_type=jnp.float32)

    @pl.when(j < n_attn_blocks)
    def _():
        scale = jnp.where(j < n_query_blocks, QUERY_SCALE, 1.0)
        oa_ref[...] = (y * scale).astype(jnp.bfloat16)

    @pl.when(j >= n_attn_blocks)
    def _():
        oh_ref[...] = y


def _proj(x, norm_w, w_in, layer):
    m, d = x.shape
    n = w_in.shape[2]
    n_attn = 3 * ATTN_WIDTH
    na = n_attn // PROJ_TN
    return pl.pallas_call(
        partial(_proj_kernel, n_attn_blocks=na, n_query_blocks=ATTN_WIDTH // PROJ_TN),
        grid=(m // PROJ_TM, n // PROJ_TN),
        in_specs=[
            pl.BlockSpec((PROJ_TM, d), lambda i, j: (i, 0)),
            pl.BlockSpec((1, d), lambda i, j: (0, 0)),
            pl.BlockSpec((None, d, PROJ_TN), lambda i, j: (layer, 0, j)),
        ],
        out_specs=[
            pl.BlockSpec((PROJ_TM, PROJ_TN), lambda i, j: (i, jnp.minimum(j, na - 1))),
            pl.BlockSpec((PROJ_TM, PROJ_TN), lambda i, j: (i, jnp.maximum(j - na, 0))),
        ],
        out_shape=[jax.ShapeDtypeStruct((m, n_attn), jnp.bfloat16),
                   jax.ShapeDtypeStruct((m, n - n_attn), jnp.float32)],
        scratch_shapes=[pltpu.VMEM((PROJ_TM, d), jnp.bfloat16)],
        compiler_params=pltpu.CompilerParams(
            dimension_semantics=("parallel", "arbitrary"), vmem_limit_bytes=VMEM_LIMIT),
        name="in_proj",
    )(x, norm_w.reshape(1, d), w_in)


def _attn_kernel(lq1_ref, lk1_ref, lq2_ref, lk2_ref, sw_ref, q_ref, k_ref, v_ref, o_ref,
                 qq_sc, vt_sc, sa_sc, sb_sc, ma_sc, mb_sc, m_sc, acc_sc, *, lambda_init):
    tq, tk = ATTN_TQ, ATTN_TK
    qi = pl.program_id(2)

    @pl.when(qi == 0)
    def _():
        ones = jnp.ones((BF16_SUBLANES, tk), vt_sc.dtype)

        def xpose(c, carry):
            start = pl.multiple_of(c * tk, tk)
            vt_sc[c] = jnp.concatenate([v_ref[0, pl.ds(start, tk), :].T, ones], axis=0)
            return carry
        lax.fori_loop(0, vt_sc.shape[0], xpose, 0)

    q = q_ref[0]
    lane = lax.broadcasted_iota(jnp.int32, q.shape, 1)
    qq_sc[:tq, :] = jnp.where(lane < ATTN_HEAD_DIM, q, jnp.zeros_like(q))
    qq_sc[tq:, :] = jnp.where(lane >= ATTN_HEAD_DIM, q, jnp.zeros_like(q))
    m_sc[...] = jnp.full_like(m_sc, -jnp.inf)
    acc_sc[...] = jnp.zeros_like(acc_sc)

    def scores(kb):
        start = pl.multiple_of(kb * tk, tk)
        s = lax.dot_general(k_ref[0, pl.ds(start, tk), :], qq_sc[...], NT_DIMS,
                            preferred_element_type=jnp.float32)
        return s, jnp.max(s, axis=0, keepdims=True)

    def consume(s, s_max, kb, masked):
        vt = vt_sc[kb]
        for n in range(2 * tq // ATTN_LANE_CHUNK):
            sl = slice(n * ATTN_LANE_CHUNK, (n + 1) * ATTN_LANE_CHUNK)
            s_n, max_n = s[:, sl], s_max[:, sl]
            if masked:
                shape = (tk, ATTN_LANE_CHUNK)
                key_pos = lax.broadcasted_iota(jnp.int32, shape, 0) + kb * tk
                q_pos = (lax.broadcasted_iota(jnp.int32, shape, 1)
                         + (qi * tq + (n * ATTN_LANE_CHUNK) % tq))
                s_n = jnp.where(key_pos <= q_pos, s_n, -jnp.inf)
                max_n = jnp.max(s_n, axis=0, keepdims=True)
            m_prev = m_sc[:, sl]
            m_new = jnp.maximum(m_prev, max_n)
            alpha = jnp.exp2(m_prev - m_new)
            p = jnp.exp2(s_n - m_new).astype(jnp.bfloat16)
            acc_sc[:, sl] = alpha * acc_sc[:, sl] + jnp.dot(vt, p,
                                                            preferred_element_type=jnp.float32)
            m_sc[:, sl] = m_new

    def group(base, masked, prefetch_next):
        s1, m1 = scores(base + 1)
        sb_sc[...], mb_sc[...] = scores(base + 2)
        consume(sa_sc[...], ma_sc[...], base, masked)
        consume(s1, m1, base + 1, masked)
        s3, m3 = scores(base + 3)
        if prefetch_next:
            sa_sc[...], ma_sc[...] = scores(base + 4)
        consume(sb_sc[...], mb_sc[...], base + 2, masked)
        consume(s3, m3, base + 3, masked)

    sa_sc[...], ma_sc[...] = scores(0)

    def body(i, carry):
        group(4 * i, masked=False, prefetch_next=True)
        return carry

    lax.fori_loop(0, qi, body, 0)
    group(4 * qi, masked=True, prefetch_next=False)

    lam = (jnp.exp(jnp.sum(lq1_ref[...] * lk1_ref[...], axis=-1, keepdims=True))
           - jnp.exp(jnp.sum(lq2_ref[...] * lk2_ref[...], axis=-1, keepdims=True))
           + lambda_init)
    acc = acc_sc[...]
    num = acc[:ATTN_V_DIM]
    den = acc[ATTN_V_DIM:ATTN_V_DIM + 1]
    o = num[:, :tq] / den[:, :tq] - lam * (num[:, tq:] / den[:, tq:])
    scale = lax.rsqrt(jnp.mean(o * o, axis=0, keepdims=True) + SUBLN_EPS)
    y = o * scale * sw_ref[...] * (1.0 - lambda_init)
    o_ref[0] = y.T.astype(o_ref.dtype)


def _attention(qkv, lq1, lk1, lq2, lk2, subln_w, lambda_init):
    b, s, _ = qkv.shape
    hd = ATTN_V_DIM
    nh = ATTN_HEADS
    small = lambda n: pl.BlockSpec((1, n), lambda bi, hi, qi: (0, 0))
    return pl.pallas_call(
        partial(_attn_kernel, lambda_init=lambda_init),
        grid=(b, nh, s // ATTN_TQ),
        in_specs=[
            small(ATTN_HEAD_DIM), small(ATTN_HEAD_DIM), small(ATTN_HEAD_DIM), small(ATTN_HEAD_DIM),
            pl.BlockSpec((hd, 1), lambda bi, hi, qi: (0, 0)),
            pl.BlockSpec((1, ATTN_TQ, hd), lambda bi, hi, qi: (bi, qi, hi)),
            pl.BlockSpec((1, s, hd), lambda bi, hi, qi: (bi, 0, nh + hi)),
            pl.BlockSpec((1, s, hd), lambda bi, hi, qi: (bi, 0, 2 * nh + hi)),
        ],
        out_specs=pl.BlockSpec((1, ATTN_TQ, hd), lambda bi, hi, qi: (bi, qi, hi)),
        out_shape=jax.ShapeDtypeStruct((b, s, ATTN_WIDTH), jnp.bfloat16),
        scratch_shapes=[pltpu.VMEM((2 * ATTN_TQ, hd), jnp.bfloat16),
                        pltpu.VMEM((s // ATTN_TK, ATTN_VT_ROWS, ATTN_TK), jnp.bfloat16),
                        pltpu.VMEM((ATTN_TK, 2 * ATTN_TQ), jnp.float32),
                        pltpu.VMEM((ATTN_TK, 2 * ATTN_TQ), jnp.float32),
                        pltpu.VMEM((1, 2 * ATTN_TQ), jnp.float32),
                        pltpu.VMEM((1, 2 * ATTN_TQ), jnp.float32),
                        pltpu.VMEM((1, 2 * ATTN_TQ), jnp.float32),
                        pltpu.VMEM((ATTN_VT_ROWS, 2 * ATTN_TQ), jnp.float32)],
        compiler_params=pltpu.CompilerParams(
            dimension_semantics=("parallel", "parallel", "arbitrary"),
            vmem_limit_bytes=VMEM_LIMIT),
        name="diff_attn",
    )(lq1.reshape(1, -1), lk1.reshape(1, -1), lq2.reshape(1, -1), lk2.reshape(1, -1),
      subln_w.reshape(-1, 1), qkv, qkv, qkv)


def _hgrn_levels(chunk):
    levels, size = [], chunk
    while size >= 2:
        levels.append(size)
        size //= 2
    return levels


def _hgrn_sum_matrix(chunk):
    x = np.arange(chunk)[:, None]
    u = np.arange(chunk)[None, :]
    secs = [u <= x, u > x]
    for size in _hgrn_levels(chunk):
        mid = (x // size) * size + size // 2 - 1
        secs.append(np.where(x > mid, (u > mid) & (u <= x), (u > x) & (u <= mid)))
    return np.concatenate(secs, axis=0).astype(np.float32)


def _hgrn_level_masks(chunk):
    x = np.arange(chunk)
    lower, same = [], []
    for size in _hgrn_levels(chunk):
        lower.append(np.broadcast_to(((x % size) >= size // 2)[:, None], (chunk, LANES)))
        same.append((x[:, None] // size) == (x[None, :] // size))
    return np.stack(lower).astype(np.float32), np.stack(same).astype(np.float32)


def _hgrn_kernel(w_ref, low_ref, same_ref, lb_ref, gw_ref, q_ref, z_ref, i_ref, g_ref, o_ref,
                 st_sc, oall_sc):
    c = HGRN_CHUNK
    t = pl.program_id(1)
    hp = pl.program_id(2)
    bf16, f32 = jnp.bfloat16, jnp.float32

    @pl.when(t == 0)
    def _():
        for hh in range(HGRN_HEADS_PER_STEP):
            st_sc[hp * HGRN_HEADS_PER_STEP + hh] = jnp.zeros((HGRN_VAL_DIM, HGRN_KEY_DIM), f32)

    def one_head(hh):
        h = hp * HGRN_HEADS_PER_STEP + hh
        sl = slice(hh * LANES, (hh + 1) * LANES)
        q = q_ref[0, :, sl]
        z = z_ref[0, :, sl]
        v32 = i_ref[0, :, sl]
        v = v32.astype(bf16)
        lb = lb_ref[:, sl]

        log_sig = jnp.minimum(z, 0.0) - jnp.log(1.0 + jnp.exp(-jnp.abs(z)))
        a0 = jnp.log(lb)
        a1 = jnp.log1p(-lb) + log_sig
        log_f = jnp.maximum(a0, a1) + jnp.log(1.0 + jnp.exp(-jnp.abs(a0 - a1)))
        kk = (1.0 - lb) * jax.nn.sigmoid(-z)

        hi = log_f.astype(bf16)
        lo = (log_f - hi.astype(f32)).astype(bf16)
        sums = jnp.dot(w_ref[...], jnp.concatenate([hi, lo], axis=1),
                       preferred_element_type=f32)
        decay = jnp.exp(sums[:, :LANES] + sums[:, LANES:])

        d_cum = decay[:c]
        d_rev = decay[c:2 * c]
        d_last = decay[c - 1:c]

        scores = jnp.zeros((c, c), f32)
        for n in range(len(_hgrn_levels(c))):
            d = decay[(2 + n) * c:(3 + n) * c]
            d_low = d * low_ref[n]
            qn = (q * d_low).astype(bf16)
            kn = (kk * (d - d_low)).astype(bf16)
            pn = lax.dot_general(qn, kn, NT_DIMS, preferred_element_type=f32)
            scores = scores + pn * same_ref[n]

        st = st_sc[h]
        o = lax.dot_general((q * d_cum).astype(bf16), st.astype(bf16), NT_DIMS,
                            preferred_element_type=f32)
        o = o + jnp.dot(scores.astype(bf16), v, preferred_element_type=f32)
        o = o + jnp.sum(q * kk, axis=-1, keepdims=True) * v32
        st_sc[h] = st * d_last + lax.dot_general(v, (kk * d_rev).astype(bf16), TN_DIMS,
                                                 preferred_element_type=f32)
        oall_sc[h] = o

    for hh in range(HGRN_HEADS_PER_STEP):
        one_head(hh)

    @pl.when(hp == HGRN_HEADS // HGRN_HEADS_PER_STEP - 1)
    def _():
        ssq = jnp.zeros((c, 1), f32)
        for hh in range(HGRN_HEADS):
            oh = oall_sc[hh]
            ssq = ssq + jnp.sum(oh * oh, axis=-1, keepdims=True)
        inv = lax.rsqrt(ssq * (1.0 / HGRN_WIDTH) + NORM_EPS)
        for hh in range(HGRN_HEADS):
            sl = slice(hh * HGRN_VAL_DIM, (hh + 1) * HGRN_VAL_DIM)
            g = g_ref[0, :, sl]
            y = oall_sc[hh] * inv * gw_ref[:, sl]
            o_ref[0, :, sl] = (y * (g * jax.nn.sigmoid(g))).astype(o_ref.dtype)


def _hgrn(hg, lb, gnorm_w):
    b, s, _ = hg.shape
    c = HGRN_CHUNK
    nh = HGRN_HEADS
    w = jnp.asarray(_hgrn_sum_matrix(c), dtype=jnp.bfloat16)
    low, same = (jnp.asarray(a) for a in _hgrn_level_masks(c))
    steps = nh // HGRN_HEADS_PER_STEP
    width = HGRN_HEADS_PER_STEP * LANES
    head_in = lambda off: pl.BlockSpec((1, c, width),
                                       lambda bi, ti, hi: (bi, ti, off * steps + hi))
    return pl.pallas_call(
        _hgrn_kernel,
        grid=(b, s // c, steps),
        in_specs=[
            pl.BlockSpec(w.shape, lambda bi, ti, hi: (0, 0)),
            pl.BlockSpec(low.shape, lambda bi, ti, hi: (0, 0, 0)),
            pl.BlockSpec(same.shape, lambda bi, ti, hi: (0, 0, 0)),
            pl.BlockSpec((1, width), lambda bi, ti, hi: (0, hi)),
            pl.BlockSpec((1, HGRN_WIDTH), lambda bi, ti, hi: (0, 0)),
            head_in(0), head_in(1), head_in(2),
            pl.BlockSpec((1, c, HGRN_WIDTH), lambda bi, ti, hi: (bi, ti, 3)),
        ],
        out_specs=pl.BlockSpec((1, c, HGRN_WIDTH), lambda bi, ti, hi: (bi, ti, 0)),
        out_shape=jax.ShapeDtypeStruct((b, s, HGRN_WIDTH), jnp.bfloat16),
        scratch_shapes=[pltpu.VMEM((nh, HGRN_VAL_DIM, HGRN_KEY_DIM), jnp.float32),
                        pltpu.VMEM((nh, c, HGRN_VAL_DIM), jnp.float32)],
        compiler_params=pltpu.CompilerParams(
            dimension_semantics=("parallel", "arbitrary", "arbitrary"),
            vmem_limit_bytes=VMEM_LIMIT),
        name="hgrn2",
    )(w, low, same, lb.reshape(1, -1), gnorm_w.reshape(1, -1), hg, hg, hg, hg)


def _outproj_kernel(x_ref, a_ref, r_ref, wa_ref, wr_ref, o_ref):
    y = jnp.dot(a_ref[...], wa_ref[...], preferred_element_type=jnp.float32)
    y = y + jnp.dot(r_ref[...], wr_ref[...], preferred_element_type=jnp.float32)
    o_ref[...] = x_ref[...] + y


def _outproj(x, attn, rec, w_out, layer):
    m, d = x.shape
    ka, kr = attn.shape[1], rec.shape[1]
    return pl.pallas_call(
        _outproj_kernel,
        grid=(m // OUT_TM,),
        in_specs=[
            pl.BlockSpec((OUT_TM, d), lambda i: (i, 0)),
            pl.BlockSpec((OUT_TM, ka), lambda i: (i, 0)),
            pl.BlockSpec((OUT_TM, kr), lambda i: (i, 0)),
            pl.BlockSpec((None, ka, d), lambda i: (layer, 0, 0)),
            pl.BlockSpec((None, kr, d), lambda i: (layer, 1, 0)),
        ],
        out_specs=pl.BlockSpec((OUT_TM, d), lambda i: (i, 0)),
        out_shape=jax.ShapeDtypeStruct((m, d), jnp.float32),
        compiler_params=pltpu.CompilerParams(
            dimension_semantics=("parallel",), vmem_limit_bytes=VMEM_LIMIT),
        name="out_proj",
    )(x, attn, rec, w_out, w_out)


def kernel(x, ffn1_norm, ffn1_w_gate, ffn1_w_up, ffn1_w_down, mix_norm, w_in, lambda_q1, lambda_k1,
           lambda_q2, lambda_k2, attn_subln, hgrn_lower_bounds, hgrn_out_norm, w_out, ffn2_norm,
           ffn2_w_gate, ffn2_w_up, ffn2_w_down, final_norm):
    b, s, d = x.shape
    lb_all = jnp.cumsum(jax.nn.softmax(hgrn_lower_bounds.astype(jnp.float32), axis=0), axis=0)
    lb_all = jnp.clip(lb_all - lb_all[0:1], 0.0, 1.0)

    ffn1 = [_cast_bf16(w) for w in (ffn1_w_gate, ffn1_w_up, ffn1_w_down)]
    ffn2 = [_cast_bf16(w) for w in (ffn2_w_gate, ffn2_w_up, ffn2_w_down)]
    w_in_bf, w_out_bf = _cast_bf16(w_in), _cast_bf16(w_out)

    xf = x.reshape(b * s, d)
    for l in range(DEPTH):
        lambda_init = 0.8 - 0.6 * math.exp(-0.3 * l)
        xf = _ffn(xf, ffn1_norm[l], *ffn1, layer=l)
        qkv, hg = _proj(xf, mix_norm[l], w_in_bf, layer=l)
        attn = _attention(qkv.reshape(b, s, -1), lambda_q1[l], lambda_k1[l], lambda_q2[l],
                          lambda_k2[l], attn_subln[l], lambda_init)
        rec = _hgrn(hg.reshape(b, s, -1), lb_all[l], hgrn_out_norm[l])
        xf = _outproj(xf, attn.reshape(b * s, -1), rec.reshape(b * s, -1), w_out_bf, layer=l)
        xf = _ffn(xf, ffn2_norm[l], *ffn2, layer=l,
                  final_w=final_norm if l == DEPTH - 1 else None)
    return xf.reshape(b, s, d)
```

```python
import math
from functools import partial

import numpy as np
import jax
import jax.numpy as jnp
from jax import lax
from jax.experimental import pallas as pl
from jax.experimental.pallas import tpu as pltpu

D_MODEL = 2048
DEPTH = 2
ATTN_WIDTH = D_MODEL // 2
HGRN_WIDTH = D_MODEL - ATTN_WIDTH
ATTN_HEADS = 8
ATTN_HEAD_DIM = ATTN_WIDTH // (2 * ATTN_HEADS)
ATTN_V_DIM = 2 * ATTN_HEAD_DIM
HGRN_HEADS = 8
HGRN_KEY_DIM = 128
HGRN_VAL_DIM = 128
D_FF = int(math.ceil(8 * D_MODEL / 3 / 256)) * 256
NORM_EPS = 1e-6
SUBLN_EPS = 1e-5

LANES = 128
V7X_VMEM_BYTES = 64 * 1024 * 1024
VMEM_LIMIT = V7X_VMEM_BYTES - 10 * 1024 * 1024

FFN_TM = 1024
FFN_ROWS = 512
FFN_TF = 512
PROJ_TM = 1024
PROJ_ROWS = 512
PROJ_TN = 1024
CAST_BLOCK_BYTES = 8 * 1024 * 1024
OUT_TM = 512
ATTN_TQ = 1024
ATTN_TK = 256
V7X_MXU_WIDTH = 256
ATTN_LANE_CHUNK = V7X_MXU_WIDTH
BF16_SUBLANES = 16
ATTN_VT_ROWS = ATTN_V_DIM + BF16_SUBLANES
QUERY_SCALE = ATTN_HEAD_DIM ** -0.5 * math.log2(math.e)
HGRN_CHUNK = 256
HGRN_HEADS_PER_STEP = 8

NT_DIMS = (((1,), (1,)), ((), ()))
TN_DIMS = (((0,), (0,)), ((), ()))


def _rms_scale(x, eps):
    return lax.rsqrt(jnp.mean(x * x, axis=-1, keepdims=True) + eps)


def _cast_kernel(w_ref, o_ref):
    o_ref[...] = w_ref[...].astype(o_ref.dtype)


def _cast_bf16(w):
    nl, r, c = w.shape
    rows = CAST_BLOCK_BYTES // (c * w.dtype.itemsize)
    rows = min(r & -r, 1 << (rows.bit_length() - 1))
    return pl.pallas_call(
        _cast_kernel,
        grid=(nl, r // rows),
        in_specs=[pl.BlockSpec((1, rows, c), lambda l, i: (l, i, 0))],
        out_specs=pl.BlockSpec((1, rows, c), lambda l, i: (l, i, 0)),
        out_shape=jax.ShapeDtypeStruct(w.shape, jnp.bfloat16),
        compiler_params=pltpu.CompilerParams(
            dimension_semantics=("parallel", "parallel"), vmem_limit_bytes=VMEM_LIMIT),
        name="cast_bf16",
    )(w)


def _ffn_kernel(*refs, final):
    if final:
        x_ref, nw_ref, wg_ref, wu_ref, wd_ref, fw_ref, o_ref, h_sc = refs
    else:
        x_ref, nw_ref, wg_ref, wu_ref, wd_ref, o_ref, h_sc = refs
    j = pl.program_id(1)

    def step(first):
        for r in range(FFN_TM // FFN_ROWS):
            rows = pl.ds(r * FFN_ROWS, FFN_ROWS)
            if first:
                x = x_ref[rows, :]
                h = (x * _rms_scale(x, NORM_EPS) * nw_ref[...]).astype(jnp.bfloat16)
                h_sc[rows, :] = h
            else:
                h = h_sc[rows, :]
            g = jnp.dot(h, wg_ref[...], preferred_element_type=jnp.float32)
            u = jnp.dot(h, wu_ref[...], preferred_element_type=jnp.float32)
            a = (g * jax.nn.sigmoid(g) * (0.5 * u)).astype(jnp.bfloat16)
            d = jnp.dot(a, wd_ref[...], preferred_element_type=jnp.float32)
            if first:
                o_ref[rows, :] = x + d
            else:
                o_ref[rows, :] += d

    @pl.when(j == 0)
    def _():
        step(first=True)

    @pl.when(j > 0)
    def _():
        step(first=False)

    if final:
        @pl.when(j == pl.num_programs(1) - 1)
        def _():
            y = o_ref[...]
            o_ref[...] = y * _rms_scale(y, NORM_EPS) * fw_ref[...]


def _ffn(x, norm_w, wg, wu, wd, layer, final_w=None):
    m, d = x.shape
    f = wg.shape[2]
    final = final_w is not None
    in_specs = [
        pl.BlockSpec((FFN_TM, d), lambda i, j: (i, 0)),
        pl.BlockSpec((1, d), lambda i, j: (0, 0)),
        pl.BlockSpec((None, d, FFN_TF), lambda i, j: (layer, 0, j)),
        pl.BlockSpec((None, d, FFN_TF), lambda i, j: (layer, 0, j)),
        pl.BlockSpec((None, FFN_TF, d), lambda i, j: (layer, j, 0)),
    ]
    args = [x, norm_w.reshape(1, d), wg, wu, wd]
    if final:
        in_specs.append(pl.BlockSpec((1, d), lambda i, j: (0, 0)))
        args.append(final_w.reshape(1, d))
    return pl.pallas_call(
        partial(_ffn_kernel, final=final),
        grid=(m // FFN_TM, f // FFN_TF),
        in_specs=in_specs,
        out_specs=pl.BlockSpec((FFN_TM, d), lambda i, j: (i, 0)),
        out_shape=jax.ShapeDtypeStruct((m, d), jnp.float32),
        scratch_shapes=[pltpu.VMEM((FFN_TM, d), jnp.bfloat16)],
        compiler_params=pltpu.CompilerParams(
            dimension_semantics=("parallel", "arbitrary"), vmem_limit_bytes=VMEM_LIMIT),
        name="ffn_final" if final else "ffn",
    )(*args)


def _proj_kernel(x_ref, nw_ref, w_ref, oa_ref, oh_ref, h_sc, *, n_attn_blocks, n_query_blocks):
    j = pl.program_id(1)

    def products(first):
        for r in range(PROJ_TM // PROJ_ROWS):
            rows = pl.ds(r * PROJ_ROWS, PROJ_ROWS)
            if first:
                x = x_ref[rows, :]
                h = (x * _rms_scale(x, NORM_EPS) * nw_ref[...]).astype(jnp.bfloat16)
                h_sc[rows, :] = h
            else:
                h = h_sc[rows, :]
            yield rows, jnp.dot(h, w_ref[...], preferred_element_type=jnp.float32)

    assert n_query_blocks >= 1

    @pl.when(j == 0)
    def _():
        for rows, y in products(first=True):
            oa_ref[rows, :] = (y * QUERY_SCALE).astype(jnp.bfloat16)

    @pl.when((j > 0) & (j < n_attn_blocks))
    def _():
        scale = jnp.where(j < n_query_blocks, QUERY_SCALE, 1.0)
        for rows, y in products(first=False):
            oa_ref[rows, :] = (y * scale).astype(jnp.bfloat16)

    @pl.when(j >= n_attn_blocks)
    def _():
        for rows, y in products(first=False):
            oh_ref[rows, :] = y


def _proj(x, norm_w, w_in, layer):
    m, d = x.shape
    n = w_in.shape[2]
    n_attn = 3 * ATTN_WIDTH
    na = n_attn // PROJ_TN
    return pl.pallas_call(
        partial(_proj_kernel, n_attn_blocks=na, n_query_blocks=ATTN_WIDTH // PROJ_TN),
        grid=(m // PROJ_TM, n // PROJ_TN),
        in_specs=[
            pl.BlockSpec((PROJ_TM, d), lambda i, j: (i, 0)),
            pl.BlockSpec((1, d), lambda i, j: (0, 0)),
            pl.BlockSpec((None, d, PROJ_TN), lambda i, j: (layer, 0, j)),
        ],
        out_specs=[
            pl.BlockSpec((PROJ_TM, PROJ_TN), lambda i, j: (i, jnp.minimum(j, na - 1))),
            pl.BlockSpec((PROJ_TM, PROJ_TN), lambda i, j: (i, jnp.maximum(j - na, 0))),
        ],
        out_shape=[jax.ShapeDtypeStruct((m, n_attn), jnp.bfloat16),
                   jax.ShapeDtypeStruct((m, n - n_attn), jnp.float32)],
        scratch_shapes=[pltpu.VMEM((PROJ_TM, d), jnp.bfloat16)],
        compiler_params=pltpu.CompilerParams(
            dimension_semantics=("parallel", "arbitrary"), vmem_limit_bytes=VMEM_LIMIT),
        name="in_proj",
    )(x, norm_w.reshape(1, d), w_in)


def _attn_kernel(lq1_ref, lk1_ref, lq2_ref, lk2_ref, sw_ref, q_ref, k_ref, v_ref, o_ref,
                 qq_sc, vt_sc, sa_sc, sb_sc, ma_sc, mb_sc, m_sc, acc_sc, *, lambda_init):
    tq, tk = ATTN_TQ, ATTN_TK
    qi = pl.program_id(2)

    @pl.when(qi == 0)
    def _():
        ones = jnp.ones((BF16_SUBLANES, 2 * tk), vt_sc.dtype)

        def xpose(c, carry):
            start = pl.multiple_of(c * 2 * tk, 2 * tk)
            vt_sc[c] = jnp.concatenate([v_ref[0, pl.ds(start, 2 * tk), :].T, ones], axis=0)
            return carry
        lax.fori_loop(0, vt_sc.shape[0], xpose, 0)

    qt = q_ref[0].T
    feat = lax.broadcasted_iota(jnp.int32, qt.shape, 0)
    qq_sc[:, :tq] = jnp.where(feat < ATTN_HEAD_DIM, qt, jnp.zeros_like(qt))
    qq_sc[:, tq:] = jnp.where(feat >= ATTN_HEAD_DIM, qt, jnp.zeros_like(qt))
    m_sc[...] = jnp.full_like(m_sc, -jnp.inf)
    acc_sc[...] = jnp.zeros_like(acc_sc)

    def scores(kb):
        start = pl.multiple_of(kb * tk, tk)
        s = jnp.dot(k_ref[0, pl.ds(start, tk), :], qq_sc[...],
                    preferred_element_type=jnp.float32)
        return s, jnp.max(s, axis=0, keepdims=True)

    def consume_pair(blocks, pair, diag_first=None):
        assert ATTN_LANE_CHUNK == tk
        vt = vt_sc[pair]
        for n in range(2 * tq // ATTN_LANE_CHUNK):
            sl = slice(n * ATTN_LANE_CHUNK, (n + 1) * ATTN_LANE_CHUNK)
            q_block = n % (tq // ATTN_LANE_CHUNK)
            parts = []
            for idx, (s, s_max) in enumerate(blocks):
                s_n, max_n = s[:, sl], s_max[:, sl]
                if diag_first is not None:
                    if q_block < diag_first + idx:
                        continue
                    if q_block == diag_first + idx:
                        shape = (tk, ATTN_LANE_CHUNK)
                        keep = (lax.broadcasted_iota(jnp.int32, shape, 0)
                                <= lax.broadcasted_iota(jnp.int32, shape, 1))
                        s_n = jnp.where(keep, s_n, -jnp.inf)
                        max_n = jnp.max(s_n, axis=0, keepdims=True)
                parts.append((s_n, max_n, idx))
            if not parts:
                continue
            m_prev = m_sc[:, sl]
            m_new = m_prev
            for _, max_n, _ in parts:
                m_new = jnp.maximum(m_new, max_n)
            alpha = jnp.exp2(m_prev - m_new)
            probs = [jnp.exp2(s_n - m_new).astype(jnp.bfloat16) for s_n, _, _ in parts]
            if len(parts) == 2:
                update = jnp.dot(vt, jnp.concatenate(probs, axis=0),
                                 preferred_element_type=jnp.float32)
            else:
                idx = parts[0][2]
                update = jnp.dot(vt[:, idx * tk:(idx + 1) * tk], probs[0],
                                 preferred_element_type=jnp.float32)
            acc_sc[:, sl] = alpha * acc_sc[:, sl] + update
            m_sc[:, sl] = m_new

    def group(base, masked, prefetch_next):
        s1, m1 = scores(base + 1)
        sb_sc[...], mb_sc[...] = scores(base + 2)
        s3, m3 = scores(base + 3)
        consume_pair([(sa_sc[...], ma_sc[...]), (s1, m1)], base // 2, 0 if masked else None)
        if prefetch_next:
            sa_sc[...], ma_sc[...] = scores(base + 4)
        consume_pair([(sb_sc[...], mb_sc[...]), (s3, m3)], base // 2 + 1, 2 if masked else None)

    sa_sc[...], ma_sc[...] = scores(0)

    def body(i, carry):
        group(4 * i, masked=False, prefetch_next=True)
        return carry

    lax.fori_loop(0, qi, body, 0)
    group(4 * qi, masked=True, prefetch_next=False)

    lam = (jnp.exp(jnp.sum(lq1_ref[...] * lk1_ref[...], axis=-1, keepdims=True))
           - jnp.exp(jnp.sum(lq2_ref[...] * lk2_ref[...], axis=-1, keepdims=True))
           + lambda_init)
    acc = acc_sc[...]
    num = acc[:ATTN_V_DIM]
    den = acc[ATTN_V_DIM:ATTN_V_DIM + 1]
    o = num[:, :tq] / den[:, :tq] - lam * (num[:, tq:] / den[:, tq:])
    scale = lax.rsqrt(jnp.mean(o * o, axis=0, keepdims=True) + SUBLN_EPS)
    y = o * scale * sw_ref[...] * (1.0 - lambda_init)
    o_ref[0] = y.T.astype(o_ref.dtype)


def _attention(qkv, lq1, lk1, lq2, lk2, subln_w, lambda_init):
    b, s, _ = qkv.shape
    hd = ATTN_V_DIM
    nh = ATTN_HEADS
    small = lambda n: pl.BlockSpec((1, n), lambda bi, hi, qi: (0, 0))
    return pl.pallas_call(
        partial(_attn_kernel, lambda_init=lambda_init),
        grid=(b, nh, s // ATTN_TQ),
        in_specs=[
            small(ATTN_HEAD_DIM), small(ATTN_HEAD_DIM), small(ATTN_HEAD_DIM), small(ATTN_HEAD_DIM),
            pl.BlockSpec((hd, 1), lambda bi, hi, qi: (0, 0)),
            pl.BlockSpec((1, ATTN_TQ, hd), lambda bi, hi, qi: (bi, qi, hi)),
            pl.BlockSpec((1, s, hd), lambda bi, hi, qi: (bi, 0, nh + hi)),
            pl.BlockSpec((1, s, hd), lambda bi, hi, qi: (bi, 0, 2 * nh + hi)),
        ],
        out_specs=pl.BlockSpec((1, ATTN_TQ, hd), lambda bi, hi, qi: (bi, qi, hi)),
        out_shape=jax.ShapeDtypeStruct((b, s, ATTN_WIDTH), jnp.bfloat16),
        scratch_shapes=[pltpu.VMEM((hd, 2 * ATTN_TQ), jnp.bfloat16),
                        pltpu.VMEM((s // (2 * ATTN_TK), ATTN_VT_ROWS, 2 * ATTN_TK), jnp.bfloat16),
                        pltpu.VMEM((ATTN_TK, 2 * ATTN_TQ), jnp.float32),
                        pltpu.VMEM((ATTN_TK, 2 * ATTN_TQ), jnp.float32),
                        pltpu.VMEM((1, 2 * ATTN_TQ), jnp.float32),
                        pltpu.VMEM((1, 2 * ATTN_TQ), jnp.float32),
                        pltpu.VMEM((1, 2 * ATTN_TQ), jnp.float32),
                        pltpu.VMEM((ATTN_VT_ROWS, 2 * ATTN_TQ), jnp.float32)],
        compiler_params=pltpu.CompilerParams(
            dimension_semantics=("parallel", "parallel", "arbitrary"),
            vmem_limit_bytes=VMEM_LIMIT),
        name="diff_attn",
    )(lq1.reshape(1, -1), lk1.reshape(1, -1), lq2.reshape(1, -1), lk2.reshape(1, -1),
      subln_w.reshape(-1, 1), qkv, qkv, qkv)


def _hgrn_levels(chunk):
    levels, size = [], chunk
    while size >= 2:
        levels.append(size)
        size //= 2
    return levels


def _hgrn_sum_matrix(chunk):
    x = np.arange(chunk)[:, None]
    u = np.arange(chunk)[None, :]
    secs = [u <= x, u > x]
    for size in _hgrn_levels(chunk):
        mid = (x // size) * size + size // 2 - 1
        secs.append(np.where(x > mid, (u > mid) & (u <= x), (u > x) & (u <= mid)))
    return np.concatenate(secs, axis=0).astype(np.float32)


def _hgrn_level_masks(chunk):
    x = np.arange(chunk)
    lower, same = [], []
    for size in _hgrn_levels(chunk):
        lower.append(np.broadcast_to(((x % size) >= size // 2)[:, None], (chunk, LANES)))
        same.append((x[:, None] // size) == (x[None, :] // size))
    return np.stack(lower).astype(np.float32), np.stack(same).astype(np.float32)


def _hgrn_kernel(w_ref, low_ref, same_ref, lb_ref, gw_ref, q_ref, z_ref, i_ref, g_ref, o_ref,
                 st_sc, oall_sc):
    c = HGRN_CHUNK
    t = pl.program_id(1)
    hp = pl.program_id(2)
    bf16, f32 = jnp.bfloat16, jnp.float32

    @pl.when(t == 0)
    def _():
        for hh in range(HGRN_HEADS_PER_STEP):
            st_sc[hp * HGRN_HEADS_PER_STEP + hh] = jnp.zeros((HGRN_VAL_DIM, HGRN_KEY_DIM), f32)

    def one_head(hh):
        h = hp * HGRN_HEADS_PER_STEP + hh
        sl = slice(hh * LANES, (hh + 1) * LANES)
        q = q_ref[0, :, sl]
        z = z_ref[0, :, sl]
        v32 = i_ref[0, :, sl]
        v = v32.astype(bf16)
        lb = lb_ref[:, sl]

        log_sig = jnp.minimum(z, 0.0) - jnp.log(1.0 + jnp.exp(-jnp.abs(z)))
        a0 = jnp.log(lb)
        a1 = jnp.log1p(-lb) + log_sig
        log_f = jnp.maximum(a0, a1) + jnp.log(1.0 + jnp.exp(-jnp.abs(a0 - a1)))
        kk = (1.0 - lb) * jax.nn.sigmoid(-z)

        hi = log_f.astype(bf16)
        lo = (log_f - hi.astype(f32)).astype(bf16)
        sums = jnp.dot(w_ref[...], jnp.concatenate([hi, lo], axis=1),
                       preferred_element_type=f32)
        decay = jnp.exp(sums[:, :LANES] + sums[:, LANES:])

        d_cum = decay[:c]
        d_rev = decay[c:2 * c]
        d_last = decay[c - 1:c]

        scores = jnp.zeros((c, c), f32)
        for n in range(len(_hgrn_levels(c))):
            d = decay[(2 + n) * c:(3 + n) * c]
            d_low = d * low_ref[n]
            qn = (q * d_low).astype(bf16)
            kn = (kk * (d - d_low)).astype(bf16)
            pn = lax.dot_general(qn, kn, NT_DIMS, preferred_element_type=f32)
            scores = scores + pn * same_ref[n]

        st = st_sc[h]
        o = lax.dot_general((q * d_cum).astype(bf16), st.astype(bf16), NT_DIMS,
                            preferred_element_type=f32)
        o = o + jnp.dot(scores.astype(bf16), v, preferred_element_type=f32)
        o = o + jnp.sum(q * kk, axis=-1, keepdims=True) * v32
        st_sc[h] = st * d_last + lax.dot_general(v, (kk * d_rev).astype(bf16), TN_DIMS,
                                                 preferred_element_type=f32)
        oall_sc[h] = o

    for hh in range(HGRN_HEADS_PER_STEP):
        one_head(hh)

    @pl.when(hp == HGRN_HEADS // HGRN_HEADS_PER_STEP - 1)
    def _():
        ssq = jnp.zeros((c, 1), f32)
        for hh in range(HGRN_HEADS):
            oh = oall_sc[hh]
            ssq = ssq + jnp.sum(oh * oh, axis=-1, keepdims=True)
        inv = lax.rsqrt(ssq * (1.0 / HGRN_WIDTH) + NORM_EPS)
        for hh in range(HGRN_HEADS):
            sl = slice(hh * HGRN_VAL_DIM, (hh + 1) * HGRN_VAL_DIM)
            g = g_ref[0, :, sl]
            y = oall_sc[hh] * inv * gw_ref[:, sl]
            o_ref[0, :, sl] = (y * (g * jax.nn.sigmoid(g))).astype(o_ref.dtype)


def _hgrn(hg, lb, gnorm_w):
    b, s, _ = hg.shape
    c = HGRN_CHUNK
    nh = HGRN_HEADS
    w = jnp.asarray(_hgrn_sum_matrix(c), dtype=jnp.bfloat16)
    low, same = (jnp.asarray(a) for a in _hgrn_level_masks(c))
    steps = nh // HGRN_HEADS_PER_STEP
    width = HGRN_HEADS_PER_STEP * LANES
    head_in = lambda off: pl.BlockSpec((1, c, width),
                                       lambda bi, ti, hi: (bi, ti, off * steps + hi))
    return pl.pallas_call(
        _hgrn_kernel,
        grid=(b, s // c, steps),
        in_specs=[
            pl.BlockSpec(w.shape, lambda bi, ti, hi: (0, 0)),
            pl.BlockSpec(low.shape, lambda bi, ti, hi: (0, 0, 0)),
            pl.BlockSpec(same.shape, lambda bi, ti, hi: (0, 0, 0)),
            pl.BlockSpec((1, width), lambda bi, ti, hi: (0, hi)),
            pl.BlockSpec((1, HGRN_WIDTH), lambda bi, ti, hi: (0, 0)),
            head_in(0), head_in(1), head_in(2),
            pl.BlockSpec((1, c, HGRN_WIDTH), lambda bi, ti, hi: (bi, ti, 3)),
        ],
        out_specs=pl.BlockSpec((1, c, HGRN_WIDTH), lambda bi, ti, hi: (bi, ti, 0)),
        out_shape=jax.ShapeDtypeStruct((b, s, HGRN_WIDTH), jnp.bfloat16),
        scratch_shapes=[pltpu.VMEM((nh, HGRN_VAL_DIM, HGRN_KEY_DIM), jnp.float32),
                        pltpu.VMEM((nh, c, HGRN_VAL_DIM), jnp.float32)],
        compiler_params=pltpu.CompilerParams(
            dimension_semantics=("parallel", "arbitrary", "arbitrary"),
            vmem_limit_bytes=VMEM_LIMIT),
        name="hgrn2",
    )(w, low, same, lb.reshape(1, -1), gnorm_w.reshape(1, -1), hg, hg, hg, hg)


def _outproj_kernel(x_ref, a_ref, r_ref, wa_ref, wr_ref, o_ref):
    y = jnp.dot(a_ref[...], wa_ref[...], preferred_element_type=jnp.float32)
    y = y + jnp.dot(r_ref[...], wr_ref[...], preferred_element_type=jnp.float32)
    o_ref[...] = x_ref[...] + y


def _outproj(x, attn, rec, w_out, layer):
    m, d = x.shape
    ka, kr = attn.shape[1], rec.shape[1]
    return pl.pallas_call(
        _outproj_kernel,
        grid=(m // OUT_TM,),
        in_specs=[
            pl.BlockSpec((OUT_TM, d), lambda i: (i, 0)),
            pl.BlockSpec((OUT_TM, ka), lambda i: (i, 0)),
            pl.BlockSpec((OUT_TM, kr), lambda i: (i, 0)),
            pl.BlockSpec((None, ka, d), lambda i: (layer, 0, 0)),
            pl.BlockSpec((None, kr, d), lambda i: (layer, 1, 0)),
        ],
        out_specs=pl.BlockSpec((OUT_TM, d), lambda i: (i, 0)),
        out_shape=jax.ShapeDtypeStruct((m, d), jnp.float32),
        compiler_params=pltpu.CompilerParams(
            dimension_semantics=("parallel",), vmem_limit_bytes=VMEM_LIMIT),
        name="out_proj",
    )(x, attn, rec, w_out, w_out)


def kernel(x, ffn1_norm, ffn1_w_gate, ffn1_w_up, ffn1_w_down, mix_norm, w_in, lambda_q1, lambda_k1,
           lambda_q2, lambda_k2, attn_subln, hgrn_lower_bounds, hgrn_out_norm, w_out, ffn2_norm,
           ffn2_w_gate, ffn2_w_up, ffn2_w_down, final_norm):
    b, s, d = x.shape
    lb_all = jnp.cumsum(jax.nn.softmax(hgrn_lower_bounds.astype(jnp.float32), axis=0), axis=0)
    lb_all = jnp.clip(lb_all - lb_all[0:1], 0.0, 1.0)

    ffn1 = [_cast_bf16(w) for w in (ffn1_w_gate, ffn1_w_up, ffn1_w_down)]
    ffn2 = [_cast_bf16(w) for w in (ffn2_w_gate, ffn2_w_up, ffn2_w_down)]
    w_in_bf, w_out_bf = _cast_bf16(w_in), _cast_bf16(w_out)

    xf = x.reshape(b * s, d)
    for l in range(DEPTH):
        lambda_init = 0.8 - 0.6 * math.exp(-0.3 * l)
        xf = _ffn(xf, ffn1_norm[l], *ffn1, layer=l)
        qkv, hg = _proj(xf, mix_norm[l], w_in_bf, layer=l)
        attn = _attention(qkv.reshape(b, s, -1), lambda_q1[l], lambda_k1[l], lambda_q2[l],
                          lambda_k2[l], attn_subln[l], lambda_init)
        rec = _hgrn(hg.reshape(b, s, -1), lb_all[l], hgrn_out_norm[l])
        xf = _outproj(xf, attn.reshape(b * s, -1), rec.reshape(b * s, -1), w_out_bf, layer=l)
        xf = _ffn(xf, ffn2_norm[l], *ffn2, layer=l,
                  final_w=final_norm if l == DEPTH - 1 else None)
    return xf.reshape(b, s, d)
```

```python
import math
from functools import partial

import numpy as np
import jax
import jax.numpy as jnp
from jax import lax
from jax.experimental import pallas as pl
from jax.experimental.pallas import tpu as pltpu

D_MODEL = 2048
DEPTH = 2
ATTN_WIDTH = D_MODEL // 2
HGRN_WIDTH = D_MODEL - ATTN_WIDTH
ATTN_HEADS = 8
ATTN_HEAD_DIM = ATTN_WIDTH // (2 * ATTN_HEADS)
ATTN_V_DIM = 2 * ATTN_HEAD_DIM
HGRN_HEADS = 8
HGRN_KEY_DIM = 128
HGRN_VAL_DIM = 128
D_FF = int(math.ceil(8 * D_MODEL / 3 / 256)) * 256
NORM_EPS = 1e-6
SUBLN_EPS = 1e-5

LANES = 128
V7X_VMEM_BYTES = 64 * 1024 * 1024
VMEM_LIMIT = V7X_VMEM_BYTES - 10 * 1024 * 1024

FFN_TM = 1024
FFN_ROWS = 512
FFN_TF = 512
PROJ_TM = 1024
PROJ_ROWS = 512
PROJ_TN = 1024
CAST_BLOCK_BYTES = 8 * 1024 * 1024
OUT_TM = 512
ATTN_TQ = 1024
ATTN_TK = 256
V7X_MXU_WIDTH = 256
ATTN_LANE_CHUNK = V7X_MXU_WIDTH
BF16_SUBLANES = 16
ATTN_VT_ROWS = ATTN_V_DIM + BF16_SUBLANES
QUERY_SCALE = ATTN_HEAD_DIM ** -0.5 * math.log2(math.e)
HGRN_CHUNK = 256
HGRN_HEADS_PER_STEP = 8

NT_DIMS = (((1,), (1,)), ((), ()))
TN_DIMS = (((0,), (0,)), ((), ()))


def _rms_scale(x, eps):
    return lax.rsqrt(jnp.mean(x * x, axis=-1, keepdims=True) + eps)


def _cast_kernel(w_ref, o_ref):
    o_ref[...] = w_ref[...].astype(o_ref.dtype)


def _cast_bf16(w):
    nl, r, c = w.shape
    rows = CAST_BLOCK_BYTES // (c * w.dtype.itemsize)
    rows = min(r & -r, 1 << (rows.bit_length() - 1))
    return pl.pallas_call(
        _cast_kernel,
        grid=(nl, r // rows),
        in_specs=[pl.BlockSpec((1, rows, c), lambda l, i: (l, i, 0))],
        out_specs=pl.BlockSpec((1, rows, c), lambda l, i: (l, i, 0)),
        out_shape=jax.ShapeDtypeStruct(w.shape, jnp.bfloat16),
        compiler_params=pltpu.CompilerParams(
            dimension_semantics=("parallel", "parallel"), vmem_limit_bytes=VMEM_LIMIT),
        name="cast_bf16",
    )(w)


def _ffn_kernel(*refs, final):
    if final:
        x_ref, nw_ref, wg_ref, wu_ref, wd_ref, fw_ref, o_ref, h_sc = refs
    else:
        x_ref, nw_ref, wg_ref, wu_ref, wd_ref, o_ref, h_sc = refs
    j = pl.program_id(1)

    def step(first):
        for r in range(FFN_TM // FFN_ROWS):
            rows = pl.ds(r * FFN_ROWS, FFN_ROWS)
            if first:
                x = x_ref[rows, :]
                h = (x * _rms_scale(x, NORM_EPS) * nw_ref[...]).astype(jnp.bfloat16)
                h_sc[rows, :] = h
            else:
                h = h_sc[rows, :]
            g = jnp.dot(h, wg_ref[...], preferred_element_type=jnp.float32)
            u = jnp.dot(h, wu_ref[...], preferred_element_type=jnp.float32)
            a = (g * jax.nn.sigmoid(g) * (0.5 * u)).astype(jnp.bfloat16)
            d = jnp.dot(a, wd_ref[...], preferred_element_type=jnp.float32)
            if first:
                o_ref[rows, :] = x + d
            else:
                o_ref[rows, :] += d

    @pl.when(j == 0)
    def _():
        step(first=True)

    @pl.when(j > 0)
    def _():
        step(first=False)

    if final:
        @pl.when(j == pl.num_programs(1) - 1)
        def _():
            y = o_ref[...]
            o_ref[...] = y * _rms_scale(y, NORM_EPS) * fw_ref[...]


def _ffn(x, norm_w, wg, wu, wd, layer, final_w=None):
    m, d = x.shape
    f = wg.shape[2]
    final = final_w is not None
    in_specs = [
        pl.BlockSpec((FFN_TM, d), lambda i, j: (i, 0)),
        pl.BlockSpec((1, d), lambda i, j: (0, 0)),
        pl.BlockSpec((None, d, FFN_TF), lambda i, j: (layer, 0, j)),
        pl.BlockSpec((None, d, FFN_TF), lambda i, j: (layer, 0, j)),
        pl.BlockSpec((None, FFN_TF, d), lambda i, j: (layer, j, 0)),
    ]
    args = [x, norm_w.reshape(1, d), wg, wu, wd]
    if final:
        in_specs.append(pl.BlockSpec((1, d), lambda i, j: (0, 0)))
        args.append(final_w.reshape(1, d))
    return pl.pallas_call(
        partial(_ffn_kernel, final=final),
        grid=(m // FFN_TM, f // FFN_TF),
        in_specs=in_specs,
        out_specs=pl.BlockSpec((FFN_TM, d), lambda i, j: (i, 0)),
        out_shape=jax.ShapeDtypeStruct((m, d), jnp.float32),
        scratch_shapes=[pltpu.VMEM((FFN_TM, d), jnp.bfloat16)],
        compiler_params=pltpu.CompilerParams(
            dimension_semantics=("parallel", "arbitrary"), vmem_limit_bytes=VMEM_LIMIT),
        name="ffn_final" if final else "ffn",
    )(*args)


def _proj_kernel(x_ref, nw_ref, w_ref, oa_ref, oh_ref, h_sc, *, n_attn_blocks, n_query_blocks):
    j = pl.program_id(1)

    def products(first):
        for r in range(PROJ_TM // PROJ_ROWS):
            rows = pl.ds(r * PROJ_ROWS, PROJ_ROWS)
            if first:
                x = x_ref[rows, :]
                h = (x * _rms_scale(x, NORM_EPS) * nw_ref[...]).astype(jnp.bfloat16)
                h_sc[rows, :] = h
            else:
                h = h_sc[rows, :]
            yield rows, jnp.dot(h, w_ref[...], preferred_element_type=jnp.float32)

    assert n_query_blocks >= 1

    @pl.when(j == 0)
    def _():
        for rows, y in products(first=True):
            oa_ref[rows, :] = (y * QUERY_SCALE).astype(jnp.bfloat16)

    @pl.when((j > 0) & (j < n_attn_blocks))
    def _():
        scale = jnp.where(j < n_query_blocks, QUERY_SCALE, 1.0)
        for rows, y in products(first=False):
            oa_ref[rows, :] = (y * scale).astype(jnp.bfloat16)

    @pl.when(j >= n_attn_blocks)
    def _():
        for rows, y in products(first=False):
            oh_ref[rows, :] = y


def _proj(x, norm_w, w_in, layer):
    m, d = x.shape
    n = w_in.shape[2]
    n_attn = 3 * ATTN_WIDTH
    na = n_attn // PROJ_TN
    return pl.pallas_call(
        partial(_proj_kernel, n_attn_blocks=na, n_query_blocks=ATTN_WIDTH // PROJ_TN),
        grid=(m // PROJ_TM, n // PROJ_TN),
        in_specs=[
            pl.BlockSpec((PROJ_TM, d), lambda i, j: (i, 0)),
            pl.BlockSpec((1, d), lambda i, j: (0, 0)),
            pl.BlockSpec((None, d, PROJ_TN), lambda i, j: (layer, 0, j)),
        ],
        out_specs=[
            pl.BlockSpec((PROJ_TM, PROJ_TN), lambda i, j: (i, jnp.minimum(j, na - 1))),
            pl.BlockSpec((PROJ_TM, PROJ_TN), lambda i, j: (i, jnp.maximum(j - na, 0))),
        ],
        out_shape=[jax.ShapeDtypeStruct((m, n_attn), jnp.bfloat16),
                   jax.ShapeDtypeStruct((m, n - n_attn), jnp.float32)],
        scratch_shapes=[pltpu.VMEM((PROJ_TM, d), jnp.bfloat16)],
        compiler_params=pltpu.CompilerParams(
            dimension_semantics=("parallel", "arbitrary"), vmem_limit_bytes=VMEM_LIMIT),
        name="in_proj",
    )(x, norm_w.reshape(1, d), w_in)


def _attn_kernel(lq1_ref, lk1_ref, lq2_ref, lk2_ref, sw_ref, q_ref, k_ref, v_ref, o_ref,
                 qq_sc, vt_sc, sa_sc, sb_sc, ma_sc, mb_sc, m_sc, acc_sc, *, lambda_init):
    tq, tk = ATTN_TQ, ATTN_TK
    qi = pl.program_id(2)

    @pl.when(qi == 0)
    def _():
        ones = jnp.ones((BF16_SUBLANES, 2 * tk), vt_sc.dtype)

        def xpose(c, carry):
            start = pl.multiple_of(c * 2 * tk, 2 * tk)
            vt_sc[c] = jnp.concatenate([v_ref[0, pl.ds(start, 2 * tk), :].T, ones], axis=0)
            return carry
        lax.fori_loop(0, vt_sc.shape[0], xpose, 0)

    q = q_ref[0]
    lane = lax.broadcasted_iota(jnp.int32, q.shape, 1)
    qq_sc[:tq, :] = jnp.where(lane < ATTN_HEAD_DIM, q, jnp.zeros_like(q))
    qq_sc[tq:, :] = jnp.where(lane >= ATTN_HEAD_DIM, q, jnp.zeros_like(q))
    m_sc[...] = jnp.full_like(m_sc, -jnp.inf)
    acc_sc[...] = jnp.zeros_like(acc_sc)

    def scores(kb):
        start = pl.multiple_of(kb * tk, tk)
        s = lax.dot_general(k_ref[0, pl.ds(start, tk), :], qq_sc[...], NT_DIMS,
                            preferred_element_type=jnp.float32)
        return s, jnp.max(s, axis=0, keepdims=True)

    def consume_pair(blocks, pair, diag_first=None):
        assert ATTN_LANE_CHUNK == tk
        vt = vt_sc[pair]
        for n in range(2 * tq // ATTN_LANE_CHUNK):
            sl = slice(n * ATTN_LANE_CHUNK, (n + 1) * ATTN_LANE_CHUNK)
            q_block = n % (tq // ATTN_LANE_CHUNK)
            parts = []
            for idx, (s, s_max) in enumerate(blocks):
                s_n, max_n = s[:, sl], s_max[:, sl]
                if diag_first is not None:
                    if q_block < diag_first + idx:
                        continue
                    if q_block == diag_first + idx:
                        shape = (tk, ATTN_LANE_CHUNK)
                        keep = (lax.broadcasted_iota(jnp.int32, shape, 0)
                                <= lax.broadcasted_iota(jnp.int32, shape, 1))
                        s_n = jnp.where(keep, s_n, -jnp.inf)
                        max_n = jnp.max(s_n, axis=0, keepdims=True)
                parts.append((s_n, max_n, idx))
            if not parts:
                continue
            m_prev = m_sc[:, sl]
            m_new = m_prev
            for _, max_n, _ in parts:
                m_new = jnp.maximum(m_new, max_n)
            alpha = jnp.exp2(m_prev - m_new)
            probs = [jnp.exp2(s_n - m_new).astype(jnp.bfloat16) for s_n, _, _ in parts]
            if len(parts) == 2:
                update = jnp.dot(vt, jnp.concatenate(probs, axis=0),
                                 preferred_element_type=jnp.float32)
            else:
                idx = parts[0][2]
                update = jnp.dot(vt[:, idx * tk:(idx + 1) * tk], probs[0],
                                 preferred_element_type=jnp.float32)
            acc_sc[:, sl] = alpha * acc_sc[:, sl] + update
            m_sc[:, sl] = m_new

    def group(base, masked, prefetch_next):
        s1, m1 = scores(base + 1)
        sb_sc[...], mb_sc[...] = scores(base + 2)
        s3, m3 = scores(base + 3)
        consume_pair([(sa_sc[...], ma_sc[...]), (s1, m1)], base // 2, 0 if masked else None)
        if prefetch_next:
            sa_sc[...], ma_sc[...] = scores(base + 4)
        consume_pair([(sb_sc[...], mb_sc[...]), (s3, m3)], base // 2 + 1, 2 if masked else None)

    sa_sc[...], ma_sc[...] = scores(0)

    def body(i, carry):
        group(4 * i, masked=False, prefetch_next=True)
        return carry

    lax.fori_loop(0, qi, body, 0)
    group(4 * qi, masked=True, prefetch_next=False)

    lam = (jnp.exp(jnp.sum(lq1_ref[...] * lk1_ref[...], axis=-1, keepdims=True))
           - jnp.exp(jnp.sum(lq2_ref[...] * lk2_ref[...], axis=-1, keepdims=True))
           + lambda_init)
    acc = acc_sc[...]
    num = acc[:ATTN_V_DIM]
    den = acc[ATTN_V_DIM:ATTN_V_DIM + 1]
    o = num[:, :tq] / den[:, :tq] - lam * (num[:, tq:] / den[:, tq:])
    scale = lax.rsqrt(jnp.mean(o * o, axis=0, keepdims=True) + SUBLN_EPS)
    y = o * scale * sw_ref[...] * (1.0 - lambda_init)
    o_ref[0] = y.T.astype(o_ref.dtype)


def _attention(qkv, lq1, lk1, lq2, lk2, subln_w, lambda_init):
    b, s, _ = qkv.shape
    hd = ATTN_V_DIM
    nh = ATTN_HEADS
    small = lambda n: pl.BlockSpec((1, n), lambda bi, hi, qi: (0, 0))
    return pl.pallas_call(
        partial(_attn_kernel, lambda_init=lambda_init),
        grid=(b, nh, s // ATTN_TQ),
        in_specs=[
            small(ATTN_HEAD_DIM), small(ATTN_HEAD_DIM), small(ATTN_HEAD_DIM), small(ATTN_HEAD_DIM),
            pl.BlockSpec((hd, 1), lambda bi, hi, qi: (0, 0)),
            pl.BlockSpec((1, ATTN_TQ, hd), lambda bi, hi, qi: (bi, qi, hi)),
            pl.BlockSpec((1, s, hd), lambda bi, hi, qi: (bi, 0, nh + hi)),
            pl.BlockSpec((1, s, hd), lambda bi, hi, qi: (bi, 0, 2 * nh + hi)),
        ],
        out_specs=pl.BlockSpec((1, ATTN_TQ, hd), lambda bi, hi, qi: (bi, qi, hi)),
        out_shape=jax.ShapeDtypeStruct((b, s, ATTN_WIDTH), jnp.bfloat16),
        scratch_shapes=[pltpu.VMEM((2 * ATTN_TQ, hd), jnp.bfloat16),
                        pltpu.VMEM((s // (2 * ATTN_TK), ATTN_VT_ROWS, 2 * ATTN_TK), jnp.bfloat16),
                        pltpu.VMEM((ATTN_TK, 2 * ATTN_TQ), jnp.float32),
                        pltpu.VMEM((ATTN_TK, 2 * ATTN_TQ), jnp.float32),
                        pltpu.VMEM((1, 2 * ATTN_TQ), jnp.float32),
                        pltpu.VMEM((1, 2 * ATTN_TQ), jnp.float32),
                        pltpu.VMEM((1, 2 * ATTN_TQ), jnp.float32),
                        pltpu.VMEM((ATTN_VT_ROWS, 2 * ATTN_TQ), jnp.float32)],
        compiler_params=pltpu.CompilerParams(
            dimension_semantics=("parallel", "parallel", "arbitrary"),
            vmem_limit_bytes=VMEM_LIMIT),
        name="diff_attn",
    )(lq1.reshape(1, -1), lk1.reshape(1, -1), lq2.reshape(1, -1), lk2.reshape(1, -1),
      subln_w.reshape(-1, 1), qkv, qkv, qkv)


def _hgrn_levels(chunk):
    levels, size = [], chunk
    while size >= 2:
        levels.append(size)
        size //= 2
    return levels


def _hgrn_sum_matrix(chunk):
    x = np.arange(chunk)[:, None]
    u = np.arange(chunk)[None, :]
    secs = [u <= x, u > x]
    for size in _hgrn_levels(chunk):
        mid = (x // size) * size + size // 2 - 1
        secs.append(np.where(x > mid, (u > mid) & (u <= x), (u > x) & (u <= mid)))
    return np.concatenate(secs, axis=0).astype(np.float32)


def _hgrn_level_masks(chunk):
    x = np.arange(chunk)
    lower, same = [], []
    for size in _hgrn_levels(chunk):
        lower.append(np.broadcast_to(((x % size) >= size // 2)[:, None], (chunk, LANES)))
        same.append((x[:, None] // size) == (x[None, :] // size))
    return np.stack(lower).astype(np.float32), np.stack(same).astype(np.float32)


def _hgrn_kernel(w_ref, low_ref, same_ref, lb_ref, q_ref, z_ref, i_ref, o_ref, st_sc):
    c = HGRN_CHUNK
    t = pl.program_id(1)
    hp = pl.program_id(2)
    bf16, f32 = jnp.bfloat16, jnp.float32

    @pl.when(t == 0)
    def _():
        for hh in range(HGRN_HEADS_PER_STEP):
            st_sc[hp * HGRN_HEADS_PER_STEP + hh] = jnp.zeros((HGRN_VAL_DIM, HGRN_KEY_DIM), f32)

    def one_head(hh):
        h = hp * HGRN_HEADS_PER_STEP + hh
        sl = slice(hh * LANES, (hh + 1) * LANES)
        q = q_ref[0, :, sl]
        z = z_ref[0, :, sl]
        v32 = i_ref[0, :, sl]
        v = v32.astype(bf16)
        lb = lb_ref[:, sl]

        log_sig = jnp.minimum(z, 0.0) - jnp.log(1.0 + jnp.exp(-jnp.abs(z)))
        a0 = jnp.log(lb)
        a1 = jnp.log1p(-lb) + log_sig
        log_f = jnp.maximum(a0, a1) + jnp.log(1.0 + jnp.exp(-jnp.abs(a0 - a1)))
        kk = (1.0 - lb) * jax.nn.sigmoid(-z)

        hi = log_f.astype(bf16)
        lo = (log_f - hi.astype(f32)).astype(bf16)
        sums = jnp.dot(w_ref[...], jnp.concatenate([hi, lo], axis=1),
                       preferred_element_type=f32)
        decay = jnp.exp(sums[:, :LANES] + sums[:, LANES:])

        d_cum = decay[:c]
        d_rev = decay[c:2 * c]
        d_last = decay[c - 1:c]

        scores = jnp.zeros((c, c), f32)
        for n in range(len(_hgrn_levels(c))):
            d = decay[(2 + n) * c:(3 + n) * c]
            d_low = d * low_ref[n]
            qn = (q * d_low).astype(bf16)
            kn = (kk * (d - d_low)).astype(bf16)
            pn = lax.dot_general(qn, kn, NT_DIMS, preferred_element_type=f32)
            scores = scores + pn * same_ref[n]

        st = st_sc[h]
        o = lax.dot_general((q * d_cum).astype(bf16), st.astype(bf16), NT_DIMS,
                            preferred_element_type=f32)
        o = o + jnp.dot(scores.astype(bf16), v, preferred_element_type=f32)
        o = o + jnp.sum(q * kk, axis=-1, keepdims=True) * v32
        st_sc[h] = st * d_last + lax.dot_general(v, (kk * d_rev).astype(bf16), TN_DIMS,
                                                 preferred_element_type=f32)
        o_ref[0, :, sl] = o

    for hh in range(HGRN_HEADS_PER_STEP):
        one_head(hh)


def _hgrn(hg, lb):
    b, s, _ = hg.shape
    c = HGRN_CHUNK
    nh = HGRN_HEADS
    w = jnp.asarray(_hgrn_sum_matrix(c), dtype=jnp.bfloat16)
    low, same = (jnp.asarray(a) for a in _hgrn_level_masks(c))
    steps = nh // HGRN_HEADS_PER_STEP
    width = HGRN_HEADS_PER_STEP * LANES
    head_in = lambda off: pl.BlockSpec((1, c, width),
                                       lambda bi, ti, hi: (bi, ti, off * steps + hi))
    return pl.pallas_call(
        _hgrn_kernel,
        grid=(b, s // c, steps),
        in_specs=[
            pl.BlockSpec(w.shape, lambda bi, ti, hi: (0, 0)),
            pl.BlockSpec(low.shape, lambda bi, ti, hi: (0, 0, 0)),
            pl.BlockSpec(same.shape, lambda bi, ti, hi: (0, 0, 0)),
            pl.BlockSpec((1, width), lambda bi, ti, hi: (0, hi)),
            head_in(0), head_in(1), head_in(2),
        ],
        out_specs=pl.BlockSpec((1, c, width), lambda bi, ti, hi: (bi, ti, hi)),
        out_shape=jax.ShapeDtypeStruct((b, s, HGRN_WIDTH), jnp.float32),
        scratch_shapes=[pltpu.VMEM((nh, HGRN_VAL_DIM, HGRN_KEY_DIM), jnp.float32)],
        compiler_params=pltpu.CompilerParams(
            dimension_semantics=("parallel", "arbitrary", "arbitrary"),
            vmem_limit_bytes=VMEM_LIMIT),
        name="hgrn2",
    )(w, low, same, lb.reshape(1, -1), hg, hg, hg)


def _outproj_kernel(x_ref, a_ref, r_ref, g_ref, gw_ref, wa_ref, wr_ref, o_ref):
    y = jnp.dot(a_ref[...], wa_ref[...], preferred_element_type=jnp.float32)
    r = r_ref[...]
    g = g_ref[...]
    rec = (r * _rms_scale(r, NORM_EPS) * gw_ref[...]) * (g * jax.nn.sigmoid(g))
    y = y + jnp.dot(rec.astype(jnp.bfloat16), wr_ref[...], preferred_element_type=jnp.float32)
    o_ref[...] = x_ref[...] + y


def _outproj(x, attn, rec, hg, gnorm_w, w_out, layer):
    m, d = x.shape
    ka, kr = attn.shape[1], rec.shape[1]
    return pl.pallas_call(
        _outproj_kernel,
        grid=(m // OUT_TM,),
        in_specs=[
            pl.BlockSpec((OUT_TM, d), lambda i: (i, 0)),
            pl.BlockSpec((OUT_TM, ka), lambda i: (i, 0)),
            pl.BlockSpec((OUT_TM, kr), lambda i: (i, 0)),
            pl.BlockSpec((OUT_TM, kr), lambda i: (i, hg.shape[1] // kr - 1)),
            pl.BlockSpec((1, kr), lambda i: (0, 0)),
            pl.BlockSpec((None, ka, d), lambda i: (layer, 0, 0)),
            pl.BlockSpec((None, kr, d), lambda i: (layer, 1, 0)),
        ],
        out_specs=pl.BlockSpec((OUT_TM, d), lambda i: (i, 0)),
        out_shape=jax.ShapeDtypeStruct((m, d), jnp.float32),
        compiler_params=pltpu.CompilerParams(
            dimension_semantics=("parallel",), vmem_limit_bytes=VMEM_LIMIT),
        name="out_proj",
    )(x, attn, rec, hg, gnorm_w.reshape(1, kr), w_out, w_out)


def kernel(x, ffn1_norm, ffn1_w_gate, ffn1_w_up, ffn1_w_down, mix_norm, w_in, lambda_q1, lambda_k1,
           lambda_q2, lambda_k2, attn_subln, hgrn_lower_bounds, hgrn_out_norm, w_out, ffn2_norm,
           ffn2_w_gate, ffn2_w_up, ffn2_w_down, final_norm):
    b, s, d = x.shape
    lb_all = jnp.cumsum(jax.nn.softmax(hgrn_lower_bounds.astype(jnp.float32), axis=0), axis=0)
    lb_all = jnp.clip(lb_all - lb_all[0:1], 0.0, 1.0)

    ffn1 = [_cast_bf16(w) for w in (ffn1_w_gate, ffn1_w_up, ffn1_w_down)]
    ffn2 = [_cast_bf16(w) for w in (ffn2_w_gate, ffn2_w_up, ffn2_w_down)]
    w_in_bf, w_out_bf = _cast_bf16(w_in), _cast_bf16(w_out)

    xf = x.reshape(b * s, d)
    for l in range(DEPTH):
        lambda_init = 0.8 - 0.6 * math.exp(-0.3 * l)
        xf = _ffn(xf, ffn1_norm[l], *ffn1, layer=l)
        qkv, hg = _proj(xf, mix_norm[l], w_in_bf, layer=l)
        attn = _attention(qkv.reshape(b, s, -1), lambda_q1[l], lambda_k1[l], lambda_q2[l],
                          lambda_k2[l], attn_subln[l], lambda_init)
        rec = _hgrn(hg.reshape(b, s, -1), lb_all[l])
        xf = _outproj(xf, attn.reshape(b * s, -1), rec.reshape(b * s, -1), hg, hgrn_out_norm[l],
                      w_out_bf, layer=l)
        xf = _ffn(xf, ffn2_norm[l], *ffn2, layer=l,
                  final_w=final_norm if l == DEPTH - 1 else None)
    return xf.reshape(b, s, d)
```
